```python
import math
import jax, jax.numpy as jnp
from jax import lax
import numpy as np

D_MODEL = 1024
BATCH = 8
SEQ = 4096
DEPTH = 1

CTX_LEN = 256
GRID_W = 64
D_MIX = D_MODEL
RET_WIDTH = D_MIX // 2
RET_HEADS = 4
RET_DK = RET_WIDTH // RET_HEADS
RET_DV = RET_DK
RET_CHUNK = 128
LRU_WIDTH = D_MIX - RET_WIDTH
LRU_BLOCKS = 8
LRU_BLOCK = LRU_WIDTH // LRU_BLOCKS
LRU_C = 8.0
CONV_W = 4
CONV_PAD_LO = 2
D_FF = 2816
N_SUB = 3
MACARON = 0.5
ALPHA = (2.0 * DEPTH) ** 0.25
BETA = (8.0 * DEPTH) ** -0.25
ROPE_BASE = 10000.0
LN_EPS = 1e-5

K_OFF = 0
V_OFF = K_OFF + RET_WIDTH
X_OFF = V_OFF + RET_WIDTH
CTX_COLS = X_OFF + LRU_WIDTH
Q_OFF = CTX_COLS
G_OFF = Q_OFF + RET_WIDTH
GATE_OFF = G_OFF + RET_WIDTH
IN_COLS = GATE_OFF + LRU_WIDTH

kernel_name = "hymba_retention_rglru_macaron_deepnorm_dit"


def layer_norm(h, g, b):
    hf = h.astype(jnp.float32)
    mu = jnp.mean(hf, -1, keepdims=True)
    var = jnp.mean(jnp.square(hf - mu), -1, keepdims=True)
    return ((hf - mu) * lax.rsqrt(var + LN_EPS) * g + b).astype(h.dtype)


def modulate(h, shift, scale):
    return h * (1 + scale) + shift


def swiglu(u, wg, wu, wd):
    return (jax.nn.silu(u @ wg) * (u @ wu)) @ wd


def ffn_sublayer(h, shift, scale, gate, wg, wu, wd, g, b):
    f = swiglu(modulate(h, shift, scale), wg, wu, wd)
    return layer_norm(ALPHA * h + MACARON * gate * f, g, b)


def _heads(t):
    B, T, _ = t.shape
    return t.reshape(B, T, RET_HEADS, -1).transpose(0, 2, 1, 3)


def _rotate(t, ang):
    cos = jnp.cos(ang).astype(t.dtype)
    sin = jnp.sin(ang).astype(t.dtype)
    t1, t2 = jnp.split(t, 2, axis=-1)
    return jnp.concatenate([t1 * cos - t2 * sin, t1 * sin + t2 * cos], -1)


def rope_2d(t, rows, cols):
    n = t.shape[-1] // 4
    inv = ROPE_BASE ** (-jnp.arange(n, dtype=jnp.float32) / n)
    a_r = rows.astype(jnp.float32)[:, None] * inv
    a_c = cols.astype(jnp.float32)[:, None] * inv
    tr, tc = jnp.split(t, 2, axis=-1)
    return jnp.concatenate([_rotate(tr, a_r), _rotate(tc, a_c)], -1)


def retention_chunkwise(q, k, v, log_g, s0, strict):
    B, H, T, dk = q.shape
    dv = v.shape[-1]
    C = RET_CHUNK
    N = T // C
    dt = q.dtype
    qc = q.reshape(B, H, N, C, dk)
    kc = k.reshape(B, H, N, C, dk)
    vc = v.reshape(B, H, N, C, dv)
    idx = jnp.arange(C)
    diff = idx[:, None] - idx[None, :]
    mask = (diff > 0) if strict else (diff >= 0)
    decay = jnp.where(mask[None], jnp.exp(log_g[:, None, None] * jnp.maximum(diff, 0)[None]), 0.0).astype(dt)
    scores = jnp.einsum('bhnid,bhnjd->bhnij', qc, kc) * decay[None, :, None]
    o = jnp.einsum('bhnij,bhnjv->bhniv', scores, vc)
    k_dec = kc * jnp.exp(log_g[:, None] * (C - 1 - idx)).astype(dt)[None, :, None, :, None]
    chunk_kv = jnp.einsum('bhnjd,bhnjv->bhndv', k_dec, vc)
    g_c = jnp.exp(log_g * C).astype(dt)[None, :, None, None]

    def step(s, kv):
        return g_c * s + kv, s

    _, s_prev = lax.scan(step, s0.astype(dt), jnp.moveaxis(chunk_kv, 2, 0))
    s_prev = jnp.moveaxis(s_prev, 0, 2)
    q_dec = qc * jnp.exp(log_g[:, None] * (idx + 1)).astype(dt)[None, :, None, :, None]
    o = o + jnp.einsum('bhnid,bhndv->bhniv', q_dec, s_prev)
    return o.reshape(B, H, T, dv)


def retention_final_state(k, v, log_g, reverse):
    T = k.shape[2]
    pos = jnp.arange(T)
    dist = pos if reverse else (T - 1 - pos)
    w = jnp.exp(log_g[:, None] * dist).astype(k.dtype)
    return jnp.einsum('bhtd,bhtv->bhdv', k * w[None, :, :, None], v)


def bidir_retention(q, k, v, log_g, s_f, s_b):
    o_f = retention_chunkwise(q, k, v, log_g[0], s_f, False)
    o_b = retention_chunkwise(jnp.flip(q, 2), jnp.flip(k, 2), jnp.flip(v, 2), log_g[1], s_b, True)
    return o_f + jnp.flip(o_b, 2)


def depthwise_conv(t, w, b):
    out = lax.conv_general_dilated(
        t, w[:, None, :], window_strides=(1,),
        padding=[(CONV_PAD_LO, CONV_W - 1 - CONV_PAD_LO)],
        dimension_numbers=('NWC', 'WIO', 'NWC'),
        feature_group_count=t.shape[-1])
    return out + b


def _lin_combine(left, right):
    return (left[0] * right[0], right[0] * left[1] + right[1])


def rglru_direction(xc, wa, ba, wi, bi, lam, h0):
    B, T, W = xc.shape
    xb = xc.reshape(B, T, LRU_BLOCKS, LRU_BLOCK)
    r = jax.nn.sigmoid(jnp.einsum('btnc,ncd->btnd', xb, wa).reshape(B, T, W) + ba).astype(jnp.float32)
    i = jax.nn.sigmoid(jnp.einsum('btnc,ncd->btnd', xb, wi).reshape(B, T, W) + bi).astype(jnp.float32)
    log_a = -LRU_C * r * jax.nn.softplus(-lam.astype(jnp.float32))
    a = jnp.exp(log_a)
    bx = jnp.sqrt(-jnp.expm1(2.0 * log_a)) * i * xc.astype(jnp.float32)
    a_cum, b_cum = lax.associative_scan(_lin_combine, (a, bx), axis=1)
    return a_cum * h0[:, None, :] + b_cum


def rglru_bidir(xc, wa, ba, wi, bi, lam, h0_f, h0_b):
    h_f = rglru_direction(xc, wa[0], ba[0], wi[0], bi[0], lam[0], h0_f)
    h_b = jnp.flip(rglru_direction(jnp.flip(xc, 1), wa[1], ba[1], wi[1], bi[1], lam[1], h0_b), 1)
    return h_f, h_b


def mixer_output(proj, o_ret, h_sum, ret_ng, ret_nb, w_out):
    B, H, T, dv = o_ret.shape
    of = o_ret.astype(jnp.float32)
    mu = jnp.mean(of, -1, keepdims=True)
    var = jnp.mean(jnp.square(of - mu), -1, keepdims=True)
    on = ((of - mu) * lax.rsqrt(var + LN_EPS)).transpose(0, 2, 1, 3).reshape(B, T, H * dv)
    ret_out = (on * ret_ng + ret_nb).astype(proj.dtype) * jax.nn.silu(proj[..., G_OFF:G_OFF + RET_WIDTH])
    lru_out = h_sum.astype(proj.dtype) * jax.nn.gelu(proj[..., GATE_OFF:GATE_OFF + LRU_WIDTH])
    return jnp.concatenate([ret_out, lru_out], -1) @ w_out


def hybrid_mixer(u_lat, u_ctx, rows, cols, need_ctx_out, w_in, w_out, ret_logit, ret_ng, ret_nb,
                 conv_w, conv_b, wa, ba, wi, bi, lam):
    log_g = jax.nn.log_sigmoid(ret_logit.astype(jnp.float32))
    k_scale = RET_DK ** -0.5
    pc = u_ctx @ (w_in if need_ctx_out else w_in[:, :CTX_COLS])
    kc = _heads(pc[..., K_OFF:K_OFF + RET_WIDTH]) * k_scale
    vc = _heads(pc[..., V_OFF:V_OFF + RET_WIDTH])
    s_cf = retention_final_state(kc, vc, log_g[0], False)
    s_cb = retention_final_state(kc, vc, log_g[1], True)
    xcc = depthwise_conv(pc[..., X_OFF:X_OFF + LRU_WIDTH], conv_w, conv_b)
    z = jnp.zeros((xcc.shape[0], LRU_WIDTH), jnp.float32)
    hcf, hcb = rglru_bidir(xcc, wa, ba, wi, bi, lam, z, z)
    pl = u_lat @ w_in
    q = rope_2d(_heads(pl[..., Q_OFF:Q_OFF + RET_WIDTH]), rows, cols)
    k = rope_2d(_heads(pl[..., K_OFF:K_OFF + RET_WIDTH]), rows, cols) * k_scale
    v = _heads(pl[..., V_OFF:V_OFF + RET_WIDTH])
    o = bidir_retention(q, k, v, log_g, s_cf, s_cb)
    xl = depthwise_conv(pl[..., X_OFF:X_OFF + LRU_WIDTH], conv_w, conv_b)
    hf, hb = rglru_bidir(xl, wa, ba, wi, bi, lam, hcf[:, -1], hcb[:, 0])
    y_lat = mixer_output(pl, o, hf + hb, ret_ng, ret_nb, w_out)
    y_ctx = None
    if need_ctx_out:
        qc = _heads(pc[..., Q_OFF:Q_OFF + RET_WIDTH])
        zs = jnp.zeros_like(s_cf)
        oc = bidir_retention(qc, kc, vc, log_g, zs, zs)
        y_ctx = mixer_output(pc, oc, hcf + hcb, ret_ng, ret_nb, w_out)
    return y_lat, y_ctx


def setup_inputs(seed: int = 0) -> dict:
    key = jax.random.key(seed)
    ks = jax.random.split(key, 32)
    f32 = jnp.float32
    L = DEPTH

    def nrm(k, shape, s):
        return jax.random.normal(k, shape, f32) * s

    gamma0 = 1.0 - 2.0 ** (-5.0 - jnp.arange(RET_HEADS, dtype=f32))
    logit0 = jnp.log(gamma0) - jnp.log1p(-gamma0)
    u = jax.random.uniform(ks[25], (L, 2, LRU_WIDTH), f32, 0.9, 0.999)
    a0 = u ** (1.0 / LRU_C)
    return {
        "x": nrm(ks[0], (BATCH, SEQ, D_MODEL), 1.0),
        "c": nrm(ks[1], (BATCH, D_MODEL), 1.0),
        "ctx": nrm(ks[2], (BATCH, CTX_LEN, D_MODEL), 1.0),
        "c_ctx": nrm(ks[3], (D_MODEL,), 1.0),
        "w_ada": nrm(ks[4], (L, D_MODEL, 3 * N_SUB * D_MODEL), D_MODEL ** -0.5),
        "b_ada": nrm(ks[5], (L, 3 * N_SUB * D_MODEL), 0.02),
        "ffn1_w_gate": nrm(ks[6], (L, D_MODEL, D_FF), D_MODEL ** -0.5),
        "ffn1_w_up": nrm(ks[7], (L, D_MODEL, D_FF), D_MODEL ** -0.5),
        "ffn1_w_down": nrm(ks[8], (L, D_FF, D_MODEL), BETA * D_FF ** -0.5),
        "ffn2_w_gate": nrm(ks[9], (L, D_MODEL, D_FF), D_MODEL ** -0.5),
        "ffn2_w_up": nrm(ks[10], (L, D_MODEL, D_FF), D_MODEL ** -0.5),
        "ffn2_w_down": nrm(ks[11], (L, D_FF, D_MODEL), BETA * D_FF ** -0.5),
        "w_in": nrm(ks[12], (L, D_MODEL, IN_COLS), D_MODEL ** -0.5),
        "w_out": nrm(ks[13], (L, D_MIX, D_MODEL), BETA * D_MIX ** -0.5),
        "ret_decay_logit": logit0 + nrm(ks[14], (L, 2, RET_HEADS), 0.1),
        "ret_norm_g": 1.0 + nrm(ks[15], (L, RET_WIDTH), 0.02),
        "ret_norm_b": nrm(ks[16], (L, RET_WIDTH), 0.02),
        "lru_conv_w": nrm(ks[17], (L, CONV_W, LRU_WIDTH), CONV_W ** -0.5),
        "lru_conv_b": nrm(ks[18], (L, LRU_WIDTH), 0.02),
        "lru_w_a": nrm(ks[19], (L, 2, LRU_BLOCKS, LRU_BLOCK, LRU_BLOCK), LRU_BLOCK ** -0.5),
        "lru_b_a": nrm(ks[20], (L, 2, LRU_WIDTH), 0.02),
        "lru_w_i": nrm(ks[21], (L, 2, LRU_BLOCKS, LRU_BLOCK, LRU_BLOCK), LRU_BLOCK ** -0.5),
        "lru_b_i": nrm(ks[22], (L, 2, LRU_WIDTH), 0.02),
        "lru_lambda": jnp.log(a0) - jnp.log1p(-a0),
        "ln_g": 1.0 + nrm(ks[23], (L, N_SUB, D_MODEL), 0.02),
        "ln_b": nrm(ks[24], (L, N_SUB, D_MODEL), 0.02),
    }


def reference(x, c, ctx, c_ctx, w_ada, b_ada, ffn1_w_gate, ffn1_w_up, ffn1_w_down,
              ffn2_w_gate, ffn2_w_up, ffn2_w_down, w_in, w_out, ret_decay_logit,
              ret_norm_g, ret_norm_b, lru_conv_w, lru_conv_b, lru_w_a, lru_b_a,
              lru_w_i, lru_b_i, lru_lambda, ln_g, ln_b):
    T = x.shape[1]
    ROWS = T // GRID_W
    rows = jnp.repeat(jnp.arange(ROWS), GRID_W)
    cols = jnp.tile(jnp.arange(GRID_W), ROWS)
    h_ctx = ctx
    for l in range(DEPTH):
        last = l == DEPTH - 1
        m_lat = jnp.split((jax.nn.silu(c) @ w_ada[l] + b_ada[l])[:, None, :], 3 * N_SUB, -1)
        m_ctx = jnp.split((jax.nn.silu(c_ctx) @ w_ada[l] + b_ada[l])[None, None, :], 3 * N_SUB, -1)
        x = ffn_sublayer(x, m_lat[0], m_lat[1], m_lat[2], ffn1_w_gate[l], ffn1_w_up[l], ffn1_w_down[l], ln_g[l, 0], ln_b[l, 0])
        h_ctx = ffn_sublayer(h_ctx, m_ctx[0], m_ctx[1], m_ctx[2], ffn1_w_gate[l], ffn1_w_up[l], ffn1_w_down[l], ln_g[l, 0], ln_b[l, 0])
        y_lat, y_ctx = hybrid_mixer(
            modulate(x, m_lat[3], m_lat[4]), modulate(h_ctx, m_ctx[3], m_ctx[4]), rows, cols, not last,
            w_in[l], w_out[l], ret_decay_logit[l], ret_norm_g[l], ret_norm_b[l],
            lru_conv_w[l], lru_conv_b[l], lru_w_a[l], lru_b_a[l], lru_w_i[l], lru_b_i[l], lru_lambda[l])
        x = layer_norm(ALPHA * x + m_lat[5] * y_lat, ln_g[l, 1], ln_b[l, 1])
        x = ffn_sublayer(x, m_lat[6], m_lat[7], m_lat[8], ffn2_w_gate[l], ffn2_w_up[l], ffn2_w_down[l], ln_g[l, 2], ln_b[l, 2])
        if not last:
            h_ctx = layer_norm(ALPHA * h_ctx + m_ctx[5] * y_ctx, ln_g[l, 1], ln_b[l, 1])
            h_ctx = ffn_sublayer(h_ctx, m_ctx[6], m_ctx[7], m_ctx[8], ffn2_w_gate[l], ffn2_w_up[l], ffn2_w_down[l], ln_g[l, 2], ln_b[l, 2])
    return x
```

```python
import functools
import math

import jax
import jax.numpy as jnp
from jax import lax
from jax.experimental import pallas as pl
from jax.experimental.pallas import tpu as pltpu

F32 = jnp.float32
BF16 = jnp.bfloat16

D_MODEL = 1024
D_FF = 2816
N_SUB = 3
RET_WIDTH = 512
RET_HEADS = 4
RET_DK = RET_WIDTH // RET_HEADS
RET_CHUNK = 128
LRU_WIDTH = 512
LRU_BLOCKS = 8
LRU_C = 8.0
CONV_W = 4
CONV_PAD_LO = 2
GRID_W = 64
DEPTH = 1
MACARON = 0.5
ALPHA = (2.0 * DEPTH) ** 0.25
ROPE_BASE = 10000.0
LN_EPS = 1e-5
K_SCALE = RET_DK ** -0.5

SEG = 512
K_SEG, V_SEG, X_SEG, Q_SEG, G_SEG, GATE_SEG = range(6)

LANES = 128
SUBLANES = 8
MXU_DIM = 256
VMEM_LIMIT_BYTES = 56 * 1024 * 1024

FF_CHUNK = MXU_DIM
N_FF_CHUNKS = D_FF // FF_CHUNK
HALO = SUBLANES


def _layer_norm(y, g, b):
    mu = jnp.mean(y, -1, keepdims=True)
    yc = y - mu
    var = jnp.mean(yc * yc, -1, keepdims=True)
    return yc * lax.rsqrt(var + LN_EPS) * g + b


def _sigmoid(x):
    return 0.5 * jnp.tanh(0.5 * x) + 0.5


def _softplus(z):
    return jnp.maximum(z, 0.0) + jnp.log(1.0 + jnp.exp(-jnp.abs(z)))


def _const_spec(shape):
    zeros = (0,) * len(shape)
    return pl.BlockSpec(shape, lambda *_: zeros, pipeline_mode=pl.Buffered(1))


def _ada_kernel(c_ref, w_ref, b_ref, o_ref):
    c = c_ref[...]
    s = c * _sigmoid(c)
    o_ref[...] = jnp.dot(s, w_ref[...], preferred_element_type=F32) + b_ref[...]


def _ada(cc, w, b):
    rows, d = cc.shape
    n = w.shape[1]
    tn = 1024
    return pl.pallas_call(
        _ada_kernel,
        out_shape=jax.ShapeDtypeStruct((rows, n), F32),
        grid=(n // tn,),
        in_specs=[pl.BlockSpec((rows, d), lambda j: (0, 0)),
                  pl.BlockSpec((d, tn), lambda j: (0, j)),
                  pl.BlockSpec((1, tn), lambda j: (0, j))],
        out_specs=pl.BlockSpec((rows, tn), lambda j: (0, j)),
        compiler_params=pltpu.CompilerParams(dimension_semantics=("arbitrary",),
                                             vmem_limit_bytes=VMEM_LIMIT_BYTES),
        name="ada",
    )(cc, w, b.reshape(1, n))


def _ffn_kernel(*refs, mod_idx, ln_idx, mix):
    if mix is None:
        x_ref, mod_ref, wgu_ref, wd_ref, lng_ref, lnb_ref, o_ref, a_scr = refs
    else:
        (x_ref, ret_ref, lru_ref, wo_ref, mod_ref, wgu_ref, wd_ref, lng_ref, lnb_ref,
         o_ref, a_scr) = refs
    m = mod_ref[0]
    h = x_ref[0]
    if mix is not None:
        gate_idx, mix_ln_idx = mix
        z = jnp.concatenate([ret_ref[0], lru_ref[0]], -1)
        y = jnp.dot(z, wo_ref[...], preferred_element_type=F32)
        h = _layer_norm(ALPHA * h + m[gate_idx:gate_idx + 1] * y,
                        lng_ref[mix_ln_idx:mix_ln_idx + 1], lnb_ref[mix_ln_idx:mix_ln_idx + 1])
    i_shift, i_scale, i_gate = mod_idx
    u = (h * (1.0 + m[i_scale:i_scale + 1]) + m[i_shift:i_shift + 1]).astype(BF16)
    for j in range(N_FF_CHUNKS):
        gp = jnp.dot(u, wgu_ref[j], preferred_element_type=F32)
        g = gp[:, :FF_CHUNK]
        p = gp[:, FF_CHUNK:]
        a_scr[:, j * FF_CHUNK:(j + 1) * FF_CHUNK] = (g * _sigmoid(g) * p).astype(BF16)
    f = jnp.dot(a_scr[...], wd_ref[...], preferred_element_type=F32)
    y = ALPHA * h + (MACARON * m[i_gate:i_gate + 1]) * f
    o_ref[0] = _layer_norm(y, lng_ref[ln_idx:ln_idx + 1], lnb_ref[ln_idx:ln_idx + 1])


def _ffn(x, mods, wgu, wd, ln_g, ln_b, *, mod_idx, ln_idx, mix=None, mix_in=None):
    bsz, t, d = x.shape
    tm = min(512, t)
    tok = lambda w: pl.BlockSpec((1, tm, w), lambda b, i: (b, i, 0))
    in_specs = [tok(d)]
    args = [x]
    if mix is not None:
        ret_out, lru_out, w_out = mix_in
        in_specs += [tok(RET_WIDTH), tok(LRU_WIDTH), _const_spec(w_out.shape)]
        args += [ret_out, lru_out, w_out]
    in_specs += [pl.BlockSpec((1, 3 * N_SUB, d), lambda b, i: (b, 0, 0)),
                 _const_spec(wgu.shape), _const_spec(wd.shape),
                 _const_spec(ln_g.shape), _const_spec(ln_b.shape)]
    args += [mods, wgu, wd, ln_g, ln_b]
    return pl.pallas_call(
        functools.partial(_ffn_kernel, mod_idx=mod_idx, ln_idx=ln_idx, mix=mix),
        out_shape=jax.ShapeDtypeStruct((bsz, t, d), F32),
        grid=(bsz, t // tm),
        in_specs=in_specs,
        out_specs=tok(d),
        scratch_shapes=[pltpu.VMEM((tm, D_FF), BF16)],
        compiler_params=pltpu.CompilerParams(dimension_semantics=("arbitrary", "arbitrary"),
                                             vmem_limit_bytes=VMEM_LIMIT_BYTES),
        name="ffn_mix" if mix is not None else "ffn",
    )(*args)


def _inproj_kernel(x_ref, mod_ref, w_ref, cos_ref, sa_ref, sb_ref,
                   k_ref, v_ref, xl_ref, q_ref, g_ref, gate_ref):
    m = mod_ref[0]
    u = (x_ref[0] * (1.0 + m[4:5]) + m[3:4]).astype(BF16)
    cos = cos_ref[...]
    sa = sa_ref[...]
    sb = sb_ref[...]

    def seg(s):
        return jnp.dot(u, w_ref[:, s * SEG:(s + 1) * SEG], preferred_element_type=F32)

    def rope(t):
        outs = []
        for hd in range(RET_HEADS):
            th = t[:, hd * RET_DK:(hd + 1) * RET_DK]
            outs.append(th * cos + pltpu.roll(th, RET_DK - 32, 1) * sa + pltpu.roll(th, 32, 1) * sb)
        return jnp.concatenate(outs, -1)

    k_ref[0] = (rope(seg(K_SEG)) * K_SCALE).astype(BF16)
    v_ref[0] = seg(V_SEG).astype(BF16)
    xl_ref[0] = seg(X_SEG)
    q_ref[0] = rope(seg(Q_SEG)).astype(BF16)
    g_ref[0] = seg(G_SEG)
    gate_ref[0] = seg(GATE_SEG)


def _inproj(x, mods, w_in, tables):
    bsz, t, d = x.shape
    tm = min(512, t)
    tok = lambda w: pl.BlockSpec((1, tm, w), lambda b, i: (b, i, 0))
    tab = pl.BlockSpec((tm, RET_DK), lambda b, i: (i, 0))
    shp = lambda dt: jax.ShapeDtypeStruct((bsz, t, SEG), dt)
    return pl.pallas_call(
        _inproj_kernel,
        out_shape=[shp(BF16), shp(BF16), shp(F32), shp(BF16), shp(F32), shp(F32)],
        grid=(bsz, t // tm),
        in_specs=[tok(d), pl.BlockSpec((1, 3 * N_SUB, d), lambda b, i: (b, 0, 0)),
                  _const_spec(w_in.shape), tab, tab, tab],
        out_specs=[tok(SEG)] * 6,
        compiler_params=pltpu.CompilerParams(dimension_semantics=("arbitrary", "arbitrary"),
                                             vmem_limit_bytes=VMEM_LIMIT_BYTES),
        name="inproj",
    )(x, mods, w_in, *tables)


def _lru_kernel(x_ref, xp_ref, xn_ref, gate_ref, cw_ref, cb_ref, w_ref, bias_ref, lam_ref, h0_ref,
                o_ref, hfl_ref, hbf_ref, hf_scr, xe_scr, a_scr, b_scr, carry_scr, *, tm, nt):
    p = pl.program_id(1)
    i = pl.program_id(2)
    tile = i + p * (nt - 1 - 2 * i)
    w = LRU_WIDTH
    ng = tm // SUBLANES

    xe_scr[0:HALO] = jnp.where(tile > 0, xp_ref[0], 0.0)
    xe_scr[HALO:HALO + tm] = x_ref[0]
    xe_scr[HALO + tm:2 * HALO + tm] = jnp.where(tile < nt - 1, xn_ref[0], 0.0)
    xc = cb_ref[...]
    for j in range(CONV_W):
        off = HALO - CONV_PAD_LO + j
        xc = xc + xe_scr[off:off + tm] * cw_ref[j:j + 1]

    gi = jnp.dot(xc.astype(BF16), w_ref[p], preferred_element_type=F32) + bias_ref[p]
    r = _sigmoid(gi[:, :w])
    ig = _sigmoid(gi[:, w:])
    log_a = (-LRU_C) * r * _softplus(-lam_ref[p])
    a = jnp.exp(log_a)
    a_scr[...] = a
    b_scr[...] = jnp.sqrt(-jnp.tanh(log_a) * (1.0 + a * a)) * ig * xc

    row = lax.broadcasted_iota(jnp.int32, (SUBLANES, w), 0)

    def group_scan(a, b, hprev, reverse):
        for k in (1, 2, 4):
            if reverse:
                sh = SUBLANES - k
                msk = row < sh
            else:
                sh = k
                msk = row >= k
            a_s = pltpu.roll(a, sh, 0)
            b_s = pltpu.roll(b, sh, 0)
            b = jnp.where(msk, a * b_s + b, b)
            a = jnp.where(msk, a * a_s, a)
        return a * hprev + b

    @pl.when(i == 0)
    def _():
        @pl.when(p == 0)
        def _():
            carry_scr[...] = jnp.broadcast_to(h0_ref[0, 0:1], (SUBLANES, w))

        @pl.when(p == 1)
        def _():
            carry_scr[...] = jnp.broadcast_to(h0_ref[0, 1:2], (SUBLANES, w))

    @pl.when(p == 0)
    def _():
        base = pl.multiple_of(tile * tm, tm)

        def body(g, hprev):
            r0 = pl.multiple_of(g * SUBLANES, SUBLANES)
            h = group_scan(a_scr[pl.ds(r0, SUBLANES)], b_scr[pl.ds(r0, SUBLANES)], hprev, False)
            hf_scr[pl.ds(base + r0, SUBLANES)] = h
            return jnp.broadcast_to(h[SUBLANES - 1:SUBLANES], (SUBLANES, w))

        hlast = lax.fori_loop(0, ng, body, carry_scr[...], unroll=4)
        carry_scr[...] = hlast

        @pl.when(i == nt - 1)
        def _():
            hfl_ref[0] = hlast[0:1]

    @pl.when(p == 1)
    def _():
        base = pl.multiple_of(tile * tm, tm)

        def body(gg, hprev):
            r0 = pl.multiple_of((ng - 1 - gg) * SUBLANES, SUBLANES)
            h = group_scan(a_scr[pl.ds(r0, SUBLANES)], b_scr[pl.ds(r0, SUBLANES)], hprev, True)
            b_scr[pl.ds(r0, SUBLANES)] = h
            return jnp.broadcast_to(h[0:1], (SUBLANES, w))

        hfirst = lax.fori_loop(0, ng, body, carry_scr[...], unroll=4)
        carry_scr[...] = hfirst

        @pl.when(i == nt - 1)
        def _():
            hbf_ref[0] = hfirst[0:1]

        hsum = hf_scr[pl.ds(base, tm)] + b_scr[...]
        o_ref[0] = (hsum * jax.nn.gelu(gate_ref[0])).astype(BF16)


def _lru(xl, gate, conv_w, conv_b, w_gates, bias, lam, h0):
    bsz, t, w = xl.shape
    tm = min(512, t)
    nt = t // tm
    hb = tm // HALO
    nhb = t // HALO
    tile = lambda p, i: i + p * (nt - 1 - 2 * i)
    tok = pl.BlockSpec((1, tm, w), lambda b, p, i: (b, tile(p, i), 0))
    prev = pl.BlockSpec((1, HALO, w), lambda b, p, i: (b, jnp.maximum(tile(p, i) * hb - 1, 0), 0))
    nxt = pl.BlockSpec((1, HALO, w),
                       lambda b, p, i: (b, jnp.minimum((tile(p, i) + 1) * hb, nhb - 1), 0))
    out = pl.BlockSpec((1, tm, w), lambda b, p, i: (b, nt - 1 - p * i, 0))
    state = pl.BlockSpec((1, 1, w), lambda b, p, i: (b, 0, 0))
    return pl.pallas_call(
        functools.partial(_lru_kernel, tm=tm, nt=nt),
        out_shape=[jax.ShapeDtypeStruct((bsz, t, w), BF16),
                   jax.ShapeDtypeStruct((bsz, 1, w), F32),
                   jax.ShapeDtypeStruct((bsz, 1, w), F32)],
        grid=(bsz, 2, nt),
        in_specs=[tok, prev, nxt, tok, _const_spec(conv_w.shape), _const_spec(conv_b.shape),
                  _const_spec(w_gates.shape), _const_spec(bias.shape), _const_spec(lam.shape),
                  pl.BlockSpec((1, 2, w), lambda b, p, i: (b, 0, 0))],
        out_specs=[out, state, state],
        scratch_shapes=[pltpu.VMEM((t, w), F32),
                        pltpu.VMEM((tm + 2 * HALO, w), F32),
                        pltpu.VMEM((tm, w), F32),
                        pltpu.VMEM((tm, w), F32),
                        pltpu.VMEM((SUBLANES, w), F32)],
        compiler_params=pltpu.CompilerParams(
            dimension_semantics=("arbitrary", "arbitrary", "arbitrary"),
            vmem_limit_bytes=VMEM_LIMIT_BYTES),
        name="lru",
    )(xl, xl, xl, gate, conv_w, conv_b, w_gates, bias, lam, h0)


def _ret_kernel(logit_ref, q_ref, k_ref, v_ref, g_ref, ng_ref, nb_ref, s0f_ref, s0b_ref,
                o_ref, sff_ref, sbf_ref, sf_scr, *, nchunks):
    hd = pl.program_id(1)
    c = RET_CHUNK
    dk = RET_DK

    def log_decay(direction):
        z = jnp.full((c, dk), logit_ref[direction, hd], F32)
        return -_softplus(-z)

    lgf = log_decay(0)
    lgb = log_decay(1)
    rowi = lax.broadcasted_iota(jnp.int32, (c, c), 0)
    coli = lax.broadcasted_iota(jnp.int32, (c, c), 1)
    diff = (rowi - coli).astype(F32)
    dmat = jnp.where(diff >= 0, jnp.exp(lgf * jnp.maximum(diff, 0.0)),
                     jnp.exp(lgb * jnp.maximum(-diff, 0.0)))
    pos = rowi.astype(F32)
    qdec_f = jnp.exp(lgf * (pos + 1.0))
    kdec_f = jnp.exp(lgf * (c - 1.0 - pos))
    qdec_b = jnp.exp(lgb * (c - pos))
    kdec_b = jnp.exp(lgb * pos)
    gc_f = jnp.exp(lgf * float(c))
    gc_b = jnp.exp(lgb * float(c))
    contract_rows = (((0,), (0,)), ((), ()))
    contract_cols = (((1,), (1,)), ((), ()))

    def chunk(ref, n):
        return ref[0, pl.ds(pl.multiple_of(n * c, c), c), :]

    def fwd_body(n, s):
        sf_scr[n] = s
        kd = (chunk(k_ref, n).astype(F32) * kdec_f).astype(BF16)
        return gc_f * s + lax.dot_general(kd, chunk(v_ref, n), contract_rows,
                                          preferred_element_type=F32)

    sff_ref[0, 0] = lax.fori_loop(0, nchunks, fwd_body, s0f_ref[0, 0])

    gn = ng_ref[...]
    bn = nb_ref[...]

    def bwd_body(nn, sb):
        n = nchunks - 1 - nn
        qn = chunk(q_ref, n)
        kn = chunk(k_ref, n)
        vn = chunk(v_ref, n)
        qf = qn.astype(F32)
        sc = lax.dot_general(qn, kn, contract_cols, preferred_element_type=F32)
        o = jnp.dot((sc * dmat).astype(BF16), vn, preferred_element_type=F32)
        o = o + jnp.dot((qf * qdec_f).astype(BF16), sf_scr[n].astype(BF16),
                        preferred_element_type=F32)
        o = o + jnp.dot((qf * qdec_b).astype(BF16), sb.astype(BF16), preferred_element_type=F32)
        mu = jnp.mean(o, -1, keepdims=True)
        oc = o - mu
        var = jnp.mean(oc * oc, -1, keepdims=True)
        on = oc * lax.rsqrt(var + LN_EPS) * gn + bn
        gg = g_ref[0, pl.ds(pl.multiple_of(n * c, c), c), :]
        o_ref[0, pl.ds(pl.multiple_of(n * c, c), c), :] = (on * (gg * _sigmoid(gg))).astype(BF16)
        kd = (kn.astype(F32) * kdec_b).astype(BF16)
        return gc_b * sb + lax.dot_general(kd, vn, contract_rows, preferred_element_type=F32)

    sbf_ref[0, 0] = lax.fori_loop(0, nchunks, bwd_body, s0b_ref[0, 0])


def _ret(logit, q, k, v, g, norm_g, norm_b, s0f, s0b):
    bsz, t, _ = q.shape
    nchunks = t // RET_CHUNK
    head = pl.BlockSpec((1, t, RET_DK), lambda b, h: (b, 0, h))
    vec = pl.BlockSpec((1, RET_DK), lambda b, h: (0, h))
    st = pl.BlockSpec((1, 1, RET_DK, RET_DK), lambda b, h: (b, h, 0, 0))
    st_shape = jax.ShapeDtypeStruct((bsz, RET_HEADS, RET_DK, RET_DK), F32)
    return pl.pallas_call(
        functools.partial(_ret_kernel, nchunks=nchunks),
        out_shape=[jax.ShapeDtypeStruct((bsz, t, RET_WIDTH), BF16), st_shape, st_shape],
        grid=(bsz, RET_HEADS),
        in_specs=[pl.BlockSpec(memory_space=pltpu.SMEM), head, head, head, head, vec, vec, st, st],
        out_specs=[head, st, st],
        scratch_shapes=[pltpu.VMEM((nchunks, RET_DK, RET_DK), F32)],
        compiler_params=pltpu.CompilerParams(dimension_semantics=("arbitrary", "arbitrary"),
                                             vmem_limit_bytes=VMEM_LIMIT_BYTES),
        name="ret",
    )(logit, q, k, v, g, norm_g, norm_b, s0f, s0b)


def _rope_tables(t):
    n = RET_DK // 4
    inv = ROPE_BASE ** (-jnp.arange(n, dtype=F32) / n)
    pos = jnp.arange(t)
    a_r = (pos // GRID_W).astype(F32)[:, None] * inv
    a_c = (pos % GRID_W).astype(F32)[:, None] * inv
    ang = jnp.concatenate([a_r, a_r, a_c, a_c], -1)
    first = (jnp.arange(RET_DK) % (2 * n)) < n
    sin = jnp.sin(ang)
    return jnp.cos(ang), jnp.where(first, -sin, 0.0), jnp.where(first, 0.0, sin)


def _block_diag(wb):
    nb, c, d = wb.shape
    eye = jnp.eye(nb, dtype=wb.dtype)
    return jnp.einsum('ncd,nm->ncmd', wb, eye).reshape(nb * c, nb * d)


def kernel(x, c, ctx, c_ctx, w_ada, b_ada, ffn1_w_gate, ffn1_w_up, ffn1_w_down, ffn2_w_gate,
           ffn2_w_up, ffn2_w_down, w_in, w_out, ret_decay_logit, ret_norm_g, ret_norm_b,
           lru_conv_w, lru_conv_b, lru_w_a, lru_b_a, lru_w_i, lru_b_i, lru_lambda, ln_g, ln_b):
    bsz, t, d = x.shape
    tc = ctx.shape[1]
    l = 0

    pad = jnp.zeros((2 * SUBLANES - bsz - 1, d), F32)
    m = _ada(jnp.concatenate([c, c_ctx[None], pad], 0), w_ada[l], b_ada[l])
    mods_lat = m[:bsz].reshape(bsz, 3 * N_SUB, d)
    mods_ctx = jnp.broadcast_to(m[bsz].reshape(1, 3 * N_SUB, d), (bsz, 3 * N_SUB, d))

    def ffn_weights(wg, wu, wd):
        wg = wg.astype(BF16).reshape(d, N_FF_CHUNKS, FF_CHUNK)
        wu = wu.astype(BF16).reshape(d, N_FF_CHUNKS, FF_CHUNK)
        wgu = jnp.concatenate([wg, wu], -1).transpose(1, 0, 2)
        return wgu, wd.astype(BF16)

    wgu1, wd1 = ffn_weights(ffn1_w_gate[l], ffn1_w_up[l], ffn1_w_down[l])
    wgu2, wd2 = ffn_weights(ffn2_w_gate[l], ffn2_w_up[l], ffn2_w_down[l])
    w_in_b = w_in[l].astype(BF16)
    w_out_b = w_out[l].astype(BF16)
    lng, lnb = ln_g[l], ln_b[l]

    w_gates = jnp.stack([jnp.concatenate([_block_diag(lru_w_a[l, dr]), _block_diag(lru_w_i[l, dr])], -1)
                         for dr in range(2)]).astype(BF16)
    gate_bias = jnp.concatenate([lru_b_a[l], lru_b_i[l]], -1)[:, None, :]
    lam = lru_lambda[l][:, None, :]
    conv_w = lru_conv_w[l]
    conv_b = lru_conv_b[l][None, :]
    norm_g = ret_norm_g[l][None, :]
    norm_b = ret_norm_b[l][None, :]
    logit = ret_decay_logit[l]

    tables_lat = _rope_tables(t)
    tables_ctx = (jnp.ones((tc, RET_DK), F32), jnp.zeros((tc, RET_DK), F32),
                  jnp.zeros((tc, RET_DK), F32))

    lru = functools.partial(_lru, conv_w=conv_w, conv_b=conv_b, w_gates=w_gates, bias=gate_bias,
                            lam=lam)

    hc = _ffn(ctx, mods_ctx, wgu1, wd1, lng, lnb, mod_idx=(0, 1, 2), ln_idx=0)
    kc, vc, xlc, qc, gc, gatec = _inproj(hc, mods_ctx, w_in_b, tables_ctx)
    _, hcf_last, hcb_first = lru(xlc, gatec, h0=jnp.zeros((bsz, 2, LRU_WIDTH), F32))
    s_zero = jnp.zeros((bsz, RET_HEADS, RET_DK, RET_DK), F32)
    _, s_cf, s_cb = _ret(logit, qc, kc, vc, gc, norm_g, norm_b, s_zero, s_zero)

    x1 = _ffn(x, mods_lat, wgu1, wd1, lng, lnb, mod_idx=(0, 1, 2), ln_idx=0)
    k, v, xl, q, g, gate = _inproj(x1, mods_lat, w_in_b, tables_lat)
    lru_out, _, _ = lru(xl, gate, h0=jnp.concatenate([hcf_last, hcb_first], 1))
    ret_out, _, _ = _ret(logit, q, k, v, g, norm_g, norm_b, s_cf, s_cb)
    return _ffn(x1, mods_lat, wgu2, wd2, lng, lnb, mod_idx=(6, 7, 8), ln_idx=2,
                mix=(5, 1), mix_in=(ret_out, lru_out, w_out_b))
```

```python
import functools
import math

import jax
import jax.numpy as jnp
from jax import lax
from jax.experimental import pallas as pl
from jax.experimental.pallas import tpu as pltpu

F32 = jnp.float32
BF16 = jnp.bfloat16

D_MODEL = 1024
D_FF = 2816
N_SUB = 3
RET_WIDTH = 512
RET_HEADS = 4
RET_DK = RET_WIDTH // RET_HEADS
RET_BLOCK = 256
RET_HEADS_PER_STEP = 2
LRU_WIDTH = 512
LRU_BLOCKS = 8
LRU_C = 8.0
CONV_W = 4
CONV_PAD_LO = 2
GRID_W = 64
DEPTH = 1
MACARON = 0.5
ALPHA = (2.0 * DEPTH) ** 0.25
ROPE_BASE = 10000.0
LN_EPS = 1e-5
K_SCALE = RET_DK ** -0.5

SEG = 512
K_SEG, V_SEG, X_SEG, Q_SEG, G_SEG, GATE_SEG = range(6)

LANES = 128
SUBLANES = 8
MXU_DIM = 256
VMEM_LIMIT_BYTES = 56 * 1024 * 1024

FF_CHUNK = MXU_DIM
N_FF_CHUNKS = D_FF // FF_CHUNK
HALO = SUBLANES


def _layer_norm(y, g, b):
    mu = jnp.mean(y, -1, keepdims=True)
    yc = y - mu
    var = jnp.mean(yc * yc, -1, keepdims=True)
    return yc * lax.rsqrt(var + LN_EPS) * g + b


def _sigmoid(x):
    return 0.5 * jnp.tanh(0.5 * x) + 0.5


def _softplus(z):
    return jnp.maximum(z, 0.0) + jnp.log(1.0 + jnp.exp(-jnp.abs(z)))


def _const_spec(shape):
    zeros = (0,) * len(shape)
    return pl.BlockSpec(shape, lambda *_: zeros, pipeline_mode=pl.Buffered(1))


def _ada_kernel(c_ref, w_ref, b_ref, o_ref):
    c = c_ref[...]
    s = c * _sigmoid(c)
    o_ref[...] = jnp.dot(s, w_ref[...], preferred_element_type=F32) + b_ref[...]


def _ada(cc, w, b):
    rows, d = cc.shape
    n = w.shape[1]
    tn = 1024
    return pl.pallas_call(
        _ada_kernel,
        out_shape=jax.ShapeDtypeStruct((rows, n), F32),
        grid=(n // tn,),
        in_specs=[pl.BlockSpec((rows, d), lambda j: (0, 0)),
                  pl.BlockSpec((d, tn), lambda j: (0, j)),
                  pl.BlockSpec((1, tn), lambda j: (0, j))],
        out_specs=pl.BlockSpec((rows, tn), lambda j: (0, j)),
        compiler_params=pltpu.CompilerParams(dimension_semantics=("arbitrary",),
                                             vmem_limit_bytes=VMEM_LIMIT_BYTES),
        name="ada",
    )(cc, w, b.reshape(1, n))


def _ffn_kernel(*refs, mod_idx, ln_idx, mix):
    if mix is None:
        x_ref, mod_ref, wgu_ref, wd_ref, lng_ref, lnb_ref, o_ref, a_scr = refs
    else:
        (x_ref, ret_ref, lru_ref, wo_ref, mod_ref, wgu_ref, wd_ref, lng_ref, lnb_ref,
         o_ref, a_scr) = refs
    m = mod_ref[0]
    h = x_ref[0]
    if mix is not None:
        gate_idx, mix_ln_idx = mix
        z = jnp.concatenate([ret_ref[0], lru_ref[0]], -1)
        y = jnp.dot(z, wo_ref[...], preferred_element_type=F32)
        h = _layer_norm(ALPHA * h + m[gate_idx:gate_idx + 1] * y,
                        lng_ref[mix_ln_idx:mix_ln_idx + 1], lnb_ref[mix_ln_idx:mix_ln_idx + 1])
    i_shift, i_scale, i_gate = mod_idx
    u = (h * (1.0 + m[i_scale:i_scale + 1]) + m[i_shift:i_shift + 1]).astype(BF16)
    for j in range(N_FF_CHUNKS):
        gp = jnp.dot(u, wgu_ref[j], preferred_element_type=F32)
        g = gp[:, :FF_CHUNK]
        p = gp[:, FF_CHUNK:]
        a_scr[:, j * FF_CHUNK:(j + 1) * FF_CHUNK] = (g * _sigmoid(g) * p).astype(BF16)
    f = jnp.dot(a_scr[...], wd_ref[...], preferred_element_type=F32)
    y = ALPHA * h + (MACARON * m[i_gate:i_gate + 1]) * f
    o_ref[0] = _layer_norm(y, lng_ref[ln_idx:ln_idx + 1], lnb_ref[ln_idx:ln_idx + 1])


def _ffn(x, mods, wgu, wd, ln_g, ln_b, *, mod_idx, ln_idx, mix=None, mix_in=None):
    bsz, t, d = x.shape
    tm = min(512, t)
    tok = lambda w: pl.BlockSpec((1, tm, w), lambda b, i: (b, i, 0))
    in_specs = [tok(d)]
    args = [x]
    if mix is not None:
        ret_out, lru_out, w_out = mix_in
        in_specs += [tok(RET_WIDTH), tok(LRU_WIDTH), _const_spec(w_out.shape)]
        args += [ret_out, lru_out, w_out]
    in_specs += [pl.BlockSpec((1, 3 * N_SUB, d), lambda b, i: (b, 0, 0)),
                 _const_spec(wgu.shape), _const_spec(wd.shape),
                 _const_spec(ln_g.shape), _const_spec(ln_b.shape)]
    args += [mods, wgu, wd, ln_g, ln_b]
    return pl.pallas_call(
        functools.partial(_ffn_kernel, mod_idx=mod_idx, ln_idx=ln_idx, mix=mix),
        out_shape=jax.ShapeDtypeStruct((bsz, t, d), F32),
        grid=(bsz, t // tm),
        in_specs=in_specs,
        out_specs=tok(d),
        scratch_shapes=[pltpu.VMEM((tm, D_FF), BF16)],
        compiler_params=pltpu.CompilerParams(dimension_semantics=("arbitrary", "arbitrary"),
                                             vmem_limit_bytes=VMEM_LIMIT_BYTES),
        name="ffn_mix" if mix is not None else "ffn",
    )(*args)


def _inproj_kernel(x_ref, mod_ref, w_ref, cos_ref, sa_ref, sb_ref,
                   k_ref, v_ref, xl_ref, q_ref, g_ref, gate_ref):
    m = mod_ref[0]
    u = (x_ref[0] * (1.0 + m[4:5]) + m[3:4]).astype(BF16)
    cos = cos_ref[...]
    sa = sa_ref[...]
    sb = sb_ref[...]

    def seg(s):
        return jnp.dot(u, w_ref[:, s * SEG:(s + 1) * SEG], preferred_element_type=F32)

    def rope(t):
        outs = []
        for hd in range(RET_HEADS):
            th = t[:, hd * RET_DK:(hd + 1) * RET_DK]
            outs.append(th * cos + pltpu.roll(th, RET_DK - 32, 1) * sa + pltpu.roll(th, 32, 1) * sb)
        return jnp.concatenate(outs, -1)

    k_ref[0] = (rope(seg(K_SEG)) * K_SCALE).astype(BF16)
    v_ref[0] = seg(V_SEG).astype(BF16)
    xl_ref[0] = seg(X_SEG)
    q_ref[0] = rope(seg(Q_SEG)).astype(BF16)
    g_ref[0] = seg(G_SEG)
    gate_ref[0] = seg(GATE_SEG)


def _inproj(x, mods, w_in, tables):
    bsz, t, d = x.shape
    tm = min(512, t)
    tok = lambda w: pl.BlockSpec((1, tm, w), lambda b, i: (b, i, 0))
    tab = pl.BlockSpec((tm, RET_DK), lambda b, i: (i, 0))
    shp = lambda dt: jax.ShapeDtypeStruct((bsz, t, SEG), dt)
    return pl.pallas_call(
        _inproj_kernel,
        out_shape=[shp(BF16), shp(BF16), shp(F32), shp(BF16), shp(F32), shp(F32)],
        grid=(bsz, t // tm),
        in_specs=[tok(d), pl.BlockSpec((1, 3 * N_SUB, d), lambda b, i: (b, 0, 0)),
                  _const_spec(w_in.shape), tab, tab, tab],
        out_specs=[tok(SEG)] * 6,
        compiler_params=pltpu.CompilerParams(dimension_semantics=("arbitrary", "arbitrary"),
                                             vmem_limit_bytes=VMEM_LIMIT_BYTES),
        name="inproj",
    )(x, mods, w_in, *tables)


def _lru_kernel(x_ref, xp_ref, xn_ref, gate_ref, cw_ref, cb_ref, w_ref, bias_ref, lam_ref, h0_ref,
                o_ref, hfl_ref, hbf_ref, hf_scr, xe_scr, a_scr, b_scr, out_scr, carry_scr, *, tm, nt):
    p = pl.program_id(1)
    i = pl.program_id(2)
    tile = i + p * (nt - 1 - 2 * i)
    w = LRU_WIDTH
    nl = w // LANES
    seg = tm // SUBLANES
    pre = CONV_PAD_LO * SUBLANES
    row = lax.broadcasted_iota(jnp.int32, (SUBLANES, w), 0)
    row1 = row[:, :LANES]

    for sl in range(nl):
        for s in range(SUBLANES):
            xe_scr[sl, pl.ds(pre + s, seg, stride=SUBLANES), :] = (
                x_ref[0, s * seg:(s + 1) * seg, sl * LANES:(sl + 1) * LANES])
    before = jnp.where(tile > 0, xp_ref[0], 0.0)
    after = jnp.where(tile < nt - 1, xn_ref[0], 0.0)
    for sl in range(nl):
        ln = slice(sl * LANES, (sl + 1) * LANES)
        for d in range(CONV_PAD_LO):
            src = pre + (seg - 1 - d) * SUBLANES
            edge = jnp.broadcast_to(before[HALO - 1 - d:HALO - d, ln], (SUBLANES, LANES))
            xe_scr[sl, pre - (d + 1) * SUBLANES:pre - d * SUBLANES, :] = jnp.where(
                row1 == 0, edge, pltpu.roll(xe_scr[sl, src:src + SUBLANES, :], 1, 0))
        edge = jnp.broadcast_to(after[0:1, ln], (SUBLANES, LANES))
        xe_scr[sl, pre + tm:pre + tm + SUBLANES, :] = jnp.where(
            row1 == SUBLANES - 1, edge,
            pltpu.roll(xe_scr[sl, pre:pre + SUBLANES, :], SUBLANES - 1, 0))

    parts = []
    for sl in range(nl):
        ln = slice(sl * LANES, (sl + 1) * LANES)
        acc = cb_ref[:, ln]
        for k in range(CONV_W):
            off = pre + (k - CONV_PAD_LO) * SUBLANES
            acc = acc + xe_scr[sl, off:off + tm, :] * cw_ref[k:k + 1, ln]
        parts.append(acc)
    xc = jnp.concatenate(parts, -1)

    gi = jnp.dot(xc.astype(BF16), w_ref[p], preferred_element_type=F32) + bias_ref[p]
    r = _sigmoid(gi[:, :w])
    ig = _sigmoid(gi[:, w:])
    log_a = (-LRU_C) * r * _softplus(-lam_ref[p])
    a = jnp.exp(log_a)
    a_scr[...] = a
    t2 = -jnp.tanh(log_a) * (1.0 + a * a)
    b_scr[...] = jnp.where(t2 > 0.0, t2 * lax.rsqrt(t2), 0.0) * ig * xc

    def sublane_scan(a, b, hprev, reverse):
        for k in (1, 2, 4):
            if reverse:
                sh = SUBLANES - k
                msk = row < sh
            else:
                sh = k
                msk = row >= k
            a_s = pltpu.roll(a, sh, 0)
            b_s = pltpu.roll(b, sh, 0)
            b = jnp.where(msk, a * b_s + b, b)
            a = jnp.where(msk, a * a_s, a)
        return a * hprev + b

    def segment_scan(reverse):
        def body(jj, carry):
            hh, pp = carry
            j = seg - 1 - jj if reverse else jj
            r0 = pl.multiple_of(j * SUBLANES, SUBLANES)
            aj = a_scr[pl.ds(r0, SUBLANES)]
            hh = aj * hh + b_scr[pl.ds(r0, SUBLANES)]
            pp = aj * pp
            b_scr[pl.ds(r0, SUBLANES)] = hh
            a_scr[pl.ds(r0, SUBLANES)] = pp
            return hh, pp

        hh, pp = lax.fori_loop(0, seg, body, (jnp.zeros((SUBLANES, w), F32),
                                              jnp.ones((SUBLANES, w), F32)), unroll=8)
        carry_in = carry_scr[...]
        ends = sublane_scan(pp, hh, carry_in, reverse)
        if reverse:
            enter = jnp.where(row == SUBLANES - 1, carry_in, pltpu.roll(ends, SUBLANES - 1, 0))
            leave = ends[0:1]
        else:
            enter = jnp.where(row == 0, carry_in, pltpu.roll(ends, 1, 0))
            leave = ends[SUBLANES - 1:SUBLANES]
        carry_scr[...] = jnp.broadcast_to(leave, (SUBLANES, w))
        h = (b_scr[...].reshape(seg, SUBLANES, w)
             + a_scr[...].reshape(seg, SUBLANES, w) * enter[None]).reshape(tm, w)
        return h, leave

    @pl.when(i == 0)
    def _():
        @pl.when(p == 0)
        def _():
            carry_scr[...] = jnp.broadcast_to(h0_ref[0, 0:1], (SUBLANES, w))

        @pl.when(p == 1)
        def _():
            carry_scr[...] = jnp.broadcast_to(h0_ref[0, 1:2], (SUBLANES, w))

    base = pl.multiple_of(tile * tm, tm)

    @pl.when(p == 0)
    def _():
        h, leave = segment_scan(False)
        hf_scr[pl.ds(base, tm)] = h

        @pl.when(i == nt - 1)
        def _():
            hfl_ref[0] = leave

    @pl.when(p == 1)
    def _():
        h, leave = segment_scan(True)

        @pl.when(i == nt - 1)
        def _():
            hbf_ref[0] = leave

        hsum = hf_scr[pl.ds(base, tm)] + h
        for sl in range(nl):
            out_scr[sl] = hsum[:, sl * LANES:(sl + 1) * LANES]
        for s in range(SUBLANES):
            ht = jnp.concatenate([out_scr[sl, pl.ds(s, seg, stride=SUBLANES), :]
                                  for sl in range(nl)], -1)
            rows = slice(s * seg, (s + 1) * seg)
            o_ref[0, rows, :] = (ht * jax.nn.gelu(gate_ref[0, rows, :])).astype(BF16)


def _lru(xl, gate, conv_w, conv_b, w_gates, bias, lam, h0):
    bsz, t, w = xl.shape
    tm = min(512, t)
    nt = t // tm
    hb = tm // HALO
    nhb = t // HALO
    tile = lambda p, i: i + p * (nt - 1 - 2 * i)
    tok = pl.BlockSpec((1, tm, w), lambda b, p, i: (b, tile(p, i), 0))
    prev = pl.BlockSpec((1, HALO, w), lambda b, p, i: (b, jnp.maximum(tile(p, i) * hb - 1, 0), 0))
    nxt = pl.BlockSpec((1, HALO, w),
                       lambda b, p, i: (b, jnp.minimum((tile(p, i) + 1) * hb, nhb - 1), 0))
    out = pl.BlockSpec((1, tm, w), lambda b, p, i: (b, nt - 1 - p * i, 0))
    state = pl.BlockSpec((1, 1, w), lambda b, p, i: (b, 0, 0))
    return pl.pallas_call(
        functools.partial(_lru_kernel, tm=tm, nt=nt),
        out_shape=[jax.ShapeDtypeStruct((bsz, t, w), BF16),
                   jax.ShapeDtypeStruct((bsz, 1, w), F32),
                   jax.ShapeDtypeStruct((bsz, 1, w), F32)],
        grid=(bsz, 2, nt),
        in_specs=[tok, prev, nxt, tok, _const_spec(conv_w.shape), _const_spec(conv_b.shape),
                  _const_spec(w_gates.shape), _const_spec(bias.shape), _const_spec(lam.shape),
                  pl.BlockSpec((1, 2, w), lambda b, p, i: (b, 0, 0))],
        out_specs=[out, state, state],
        scratch_shapes=[pltpu.VMEM((t, w), F32),
                        pltpu.VMEM((w // LANES, tm + (CONV_W - 1) * SUBLANES, LANES), F32),
                        pltpu.VMEM((tm, w), F32),
                        pltpu.VMEM((tm, w), F32),
                        pltpu.VMEM((w // LANES, tm, LANES), F32),
                        pltpu.VMEM((SUBLANES, w), F32)],
        compiler_params=pltpu.CompilerParams(
            dimension_semantics=("arbitrary", "arbitrary", "arbitrary"),
            vmem_limit_bytes=VMEM_LIMIT_BYTES),
        name="lru",
    )(xl, xl, xl, gate, conv_w, conv_b, w_gates, bias, lam, h0)


def _ret_kernel(logit_ref, q_ref, k_ref, v_ref, g_ref, ng_ref, nb_ref, s0f_ref, s0b_ref,
                o_ref, sff_ref, sbf_ref, dmat_scr, kv_scr, ss_scr, *, nchunks):
    hp = pl.program_id(1)
    c = RET_BLOCK
    dk = RET_DK
    contract_cols = (((1,), (1,)), ((), ()))

    def rows(n):
        return pl.ds(pl.multiple_of(n * c, c), c)

    def log_decay(direction, hd, shape):
        z = jnp.full(shape, logit_ref[direction, hd], F32)
        return -_softplus(-z)

    rowi = lax.broadcasted_iota(jnp.int32, (c, c), 0)
    coli = lax.broadcasted_iota(jnp.int32, (c, c), 1)
    diff = (rowi - coli).astype(F32)
    pos = lax.broadcasted_iota(jnp.int32, (c, dk), 0).astype(F32)
    lpos = lax.broadcasted_iota(jnp.int32, (dk, c), 1).astype(F32)

    heads = []
    for hh in range(RET_HEADS_PER_STEP):
        hd = hp * RET_HEADS_PER_STEP + hh
        lgf = log_decay(0, hd, (c, c))
        lgb = log_decay(1, hd, (c, c))
        dmat_scr[hh] = jnp.where(diff >= 0, jnp.exp(lgf * jnp.maximum(diff, 0.0)),
                                 jnp.exp(lgb * jnp.maximum(-diff, 0.0)))
        lgf_r, lgb_r = log_decay(0, hd, (c, dk)), log_decay(1, hd, (c, dk))
        heads.append(dict(
            lanes=slice(hh * dk, (hh + 1) * dk),
            qdec_f=jnp.exp(lgf_r * (pos + 1.0)), qdec_b=jnp.exp(lgb_r * (c - pos)),
            kdec_f=jnp.exp(log_decay(0, hd, (dk, c)) * (c - 1.0 - lpos)),
            kdec_b=jnp.exp(log_decay(1, hd, (dk, c)) * lpos),
            gc_f=jnp.exp(log_decay(0, hd, (dk, dk)) * float(c)),
            gc_b=jnp.exp(log_decay(1, hd, (dk, dk)) * float(c))))

    def pass_a(n, carry):
        for hh, hv in enumerate(heads):
            kt = k_ref[0, rows(n), hv["lanes"]].astype(F32).T
            lhs = jnp.concatenate([kt * hv["kdec_f"], kt * hv["kdec_b"]], 0).astype(BF16)
            kv_scr[hh, n] = jnp.dot(lhs, v_ref[0, rows(n), hv["lanes"]],
                                    preferred_element_type=F32)
        return carry

    lax.fori_loop(0, nchunks, pass_a, 0, unroll=2 if nchunks > 1 else 1)

    for hh, hv in enumerate(heads):
        def pass_s(n, carry, hh=hh, hv=hv):
            sf, sb = carry
            m = nchunks - 1 - n
            ss_scr[hh, n, 0:dk] = sf.astype(BF16)
            ss_scr[hh, m, dk:2 * dk] = sb.astype(BF16)
            return (hv["gc_f"] * sf + kv_scr[hh, n, 0:dk],
                    hv["gc_b"] * sb + kv_scr[hh, m, dk:2 * dk])

        sf, sb = lax.fori_loop(0, nchunks, pass_s, (s0f_ref[0, hh], s0b_ref[0, hh]))
        sff_ref[0, hh] = sf
        sbf_ref[0, hh] = sb

    gn = ng_ref[...]
    bn = nb_ref[...]

    def pass_b(n, carry):
        for hh, hv in enumerate(heads):
            ln = hv["lanes"]
            qn = q_ref[0, rows(n), ln]
            sc = lax.dot_general(qn, k_ref[0, rows(n), ln], contract_cols,
                                 preferred_element_type=F32)
            o = jnp.dot((sc * dmat_scr[hh]).astype(BF16), v_ref[0, rows(n), ln],
                        preferred_element_type=F32)
            qf = qn.astype(F32)
            qd = jnp.concatenate([(qf * hv["qdec_f"]).astype(BF16),
                                  (qf * hv["qdec_b"]).astype(BF16)], -1)
            o = o + jnp.dot(qd, ss_scr[hh, n], preferred_element_type=F32)
            mu = jnp.mean(o, -1, keepdims=True)
            oc = o - mu
            var = jnp.mean(oc * oc, -1, keepdims=True)
            on = oc * lax.rsqrt(var + LN_EPS) * gn[:, ln] + bn[:, ln]
            gg = g_ref[0, rows(n), ln]
            o_ref[0, rows(n), ln] = (on * (gg * _sigmoid(gg))).astype(BF16)
        return carry

    lax.fori_loop(0, nchunks, pass_b, 0, unroll=2 if nchunks > 1 else 1)


def _ret(logit, q, k, v, g, norm_g, norm_b, s0f, s0b):
    bsz, t, _ = q.shape
    nchunks = t // RET_BLOCK
    hps = RET_HEADS_PER_STEP
    wid = hps * RET_DK
    head = pl.BlockSpec((1, t, wid), lambda b, h: (b, 0, h))
    vec = pl.BlockSpec((1, wid), lambda b, h: (0, h))
    st = pl.BlockSpec((1, hps, RET_DK, RET_DK), lambda b, h: (b, h, 0, 0))
    st_shape = jax.ShapeDtypeStruct((bsz, RET_HEADS, RET_DK, RET_DK), F32)
    return pl.pallas_call(
        functools.partial(_ret_kernel, nchunks=nchunks),
        out_shape=[jax.ShapeDtypeStruct((bsz, t, RET_WIDTH), BF16), st_shape, st_shape],
        grid=(bsz, RET_HEADS // hps),
        in_specs=[pl.BlockSpec(memory_space=pltpu.SMEM), head, head, head, head, vec, vec, st, st],
        out_specs=[head, st, st],
        scratch_shapes=[pltpu.VMEM((hps, RET_BLOCK, RET_BLOCK), F32),
                        pltpu.VMEM((hps, nchunks, 2 * RET_DK, RET_DK), F32),
                        pltpu.VMEM((hps, nchunks, 2 * RET_DK, RET_DK), BF16)],
        compiler_params=pltpu.CompilerParams(dimension_semantics=("arbitrary", "arbitrary"),
                                             vmem_limit_bytes=VMEM_LIMIT_BYTES),
        name="ret",
    )(logit, q, k, v, g, norm_g, norm_b, s0f, s0b)


def _rope_tables(t):
    n = RET_DK // 4
    inv = ROPE_BASE ** (-jnp.arange(n, dtype=F32) / n)
    pos = jnp.arange(t)
    a_r = (pos // GRID_W).astype(F32)[:, None] * inv
    a_c = (pos % GRID_W).astype(F32)[:, None] * inv
    ang = jnp.concatenate([a_r, a_r, a_c, a_c], -1)
    first = (jnp.arange(RET_DK) % (2 * n)) < n
    sin = jnp.sin(ang)
    return jnp.cos(ang), jnp.where(first, -sin, 0.0), jnp.where(first, 0.0, sin)


def _block_diag(wb):
    nb, c, d = wb.shape
    eye = jnp.eye(nb, dtype=wb.dtype)
    return jnp.einsum('ncd,nm->ncmd', wb, eye).reshape(nb * c, nb * d)


def kernel(x, c, ctx, c_ctx, w_ada, b_ada, ffn1_w_gate, ffn1_w_up, ffn1_w_down, ffn2_w_gate,
           ffn2_w_up, ffn2_w_down, w_in, w_out, ret_decay_logit, ret_norm_g, ret_norm_b,
           lru_conv_w, lru_conv_b, lru_w_a, lru_b_a, lru_w_i, lru_b_i, lru_lambda, ln_g, ln_b):
    bsz, t, d = x.shape
    tc = ctx.shape[1]
    l = 0

    pad = jnp.zeros((2 * SUBLANES - bsz - 1, d), F32)
    m = _ada(jnp.concatenate([c, c_ctx[None], pad], 0), w_ada[l], b_ada[l])
    mods_lat = m[:bsz].reshape(bsz, 3 * N_SUB, d)
    mods_ctx = jnp.broadcast_to(m[bsz].reshape(1, 3 * N_SUB, d), (bsz, 3 * N_SUB, d))

    def ffn_weights(wg, wu, wd):
        wg = wg.astype(BF16).reshape(d, N_FF_CHUNKS, FF_CHUNK)
        wu = wu.astype(BF16).reshape(d, N_FF_CHUNKS, FF_CHUNK)
        wgu = jnp.concatenate([wg, wu], -1).transpose(1, 0, 2)
        return wgu, wd.astype(BF16)

    wgu1, wd1 = ffn_weights(ffn1_w_gate[l], ffn1_w_up[l], ffn1_w_down[l])
    wgu2, wd2 = ffn_weights(ffn2_w_gate[l], ffn2_w_up[l], ffn2_w_down[l])
    w_in_b = w_in[l].astype(BF16)
    w_out_b = w_out[l].astype(BF16)
    lng, lnb = ln_g[l], ln_b[l]

    w_gates = jnp.stack([jnp.concatenate([_block_diag(lru_w_a[l, dr]), _block_diag(lru_w_i[l, dr])], -1)
                         for dr in range(2)]).astype(BF16)
    gate_bias = jnp.concatenate([lru_b_a[l], lru_b_i[l]], -1)[:, None, :]
    lam = lru_lambda[l][:, None, :]
    conv_w = lru_conv_w[l]
    conv_b = lru_conv_b[l][None, :]
    norm_g = ret_norm_g[l][None, :]
    norm_b = ret_norm_b[l][None, :]
    logit = ret_decay_logit[l]

    tables_lat = _rope_tables(t)
    tables_ctx = (jnp.ones((tc, RET_DK), F32), jnp.zeros((tc, RET_DK), F32),
                  jnp.zeros((tc, RET_DK), F32))

    lru = functools.partial(_lru, conv_w=conv_w, conv_b=conv_b, w_gates=w_gates, bias=gate_bias,
                            lam=lam)

    hc = _ffn(ctx, mods_ctx, wgu1, wd1, lng, lnb, mod_idx=(0, 1, 2), ln_idx=0)
    kc, vc, xlc, qc, gc, gatec = _inproj(hc, mods_ctx, w_in_b, tables_ctx)
    _, hcf_last, hcb_first = lru(xlc, gatec, h0=jnp.zeros((bsz, 2, LRU_WIDTH), F32))
    s_zero = jnp.zeros((bsz, RET_HEADS, RET_DK, RET_DK), F32)
    _, s_cf, s_cb = _ret(logit, qc, kc, vc, gc, norm_g, norm_b, s_zero, s_zero)

    x1 = _ffn(x, mods_lat, wgu1, wd1, lng, lnb, mod_idx=(0, 1, 2), ln_idx=0)
    k, v, xl, q, g, gate = _inproj(x1, mods_lat, w_in_b, tables_lat)
    lru_out, _, _ = lru(xl, gate, h0=jnp.concatenate([hcf_last, hcb_first], 1))
    ret_out, _, _ = _ret(logit, q, k, v, g, norm_g, norm_b, s_cf, s_cb)
    return _ffn(x1, mods_lat, wgu2, wd2, lng, lnb, mod_idx=(6, 7, 8), ln_idx=2,
                mix=(5, 1), mix_in=(ret_out, lru_out, w_out_b))
```

```python
import functools

import jax
import jax.numpy as jnp
from jax import lax
from jax.experimental import pallas as pl
from jax.experimental.pallas import tpu as pltpu

F32 = jnp.float32
BF16 = jnp.bfloat16

D_MODEL = 1024
D_FF = 2816
N_SUB = 3
RET_WIDTH = 512
RET_HEADS = 4
RET_DK = RET_WIDTH // RET_HEADS
RET_BLOCK = 256
RET_HEADS_PER_STEP = 2
LRU_WIDTH = 512
LRU_BLOCKS = 8
LRU_C = 8.0
CONV_W = 4
CONV_PAD_LO = 2
GRID_W = 64
DEPTH = 1
MACARON = 0.5
ALPHA = (2.0 * DEPTH) ** 0.25
ROPE_BASE = 10000.0
LN_EPS = 1e-5
K_SCALE = RET_DK ** -0.5

SEG = 512
K_SEG, V_SEG, X_SEG, Q_SEG, G_SEG, GATE_SEG = range(6)

LANES = 128
SUBLANES = 8
BF16_ROWS = 16
MXU_DIM = 256
VMEM_LIMIT_BYTES = 56 * 1024 * 1024

FF_CHUNK = MXU_DIM
N_FF_CHUNKS = D_FF // FF_CHUNK
TOKEN_TILE = 512
FFN_SUB_TILE = 512
ADA_COL_TILE = 1024
HALO = SUBLANES
LRU_PRE = CONV_PAD_LO * SUBLANES
LRU_EXT = (CONV_W - 1) * SUBLANES


def _layer_norm(y, g, b):
    mu = jnp.mean(y, -1, keepdims=True)
    yc = y - mu
    var = jnp.mean(yc * yc, -1, keepdims=True)
    return yc * lax.rsqrt(var + LN_EPS) * g + b


def _sigmoid(x):
    return 0.5 * jnp.tanh(0.5 * x) + 0.5


def _softplus(z):
    return jnp.maximum(z, 0.0) + jnp.log(1.0 + jnp.exp(-jnp.abs(z)))


def _const_spec(shape):
    zeros = (0,) * len(shape)
    return pl.BlockSpec(shape, lambda *_: zeros, pipeline_mode=pl.Buffered(1))


def _params(n_grid_axes):
    return pltpu.CompilerParams(dimension_semantics=("arbitrary",) * n_grid_axes,
                                vmem_limit_bytes=VMEM_LIMIT_BYTES)


def _ada_kernel(c_ref, w_ref, b_ref, o_ref):
    c = c_ref[...]
    s = c * _sigmoid(c)
    o_ref[...] = jnp.dot(s, w_ref[...], preferred_element_type=F32) + b_ref[...]


def _ada(cc, w, b):
    rows, d = cc.shape
    n = w.shape[1]
    tn = ADA_COL_TILE
    return pl.pallas_call(
        _ada_kernel,
        out_shape=jax.ShapeDtypeStruct((rows, n), F32),
        grid=(n // tn,),
        in_specs=[pl.BlockSpec((rows, d), lambda j: (0, 0)),
                  pl.BlockSpec((d, tn), lambda j: (0, j)),
                  pl.BlockSpec((1, tn), lambda j: (0, j))],
        out_specs=pl.BlockSpec((rows, tn), lambda j: (0, j)),
        compiler_params=_params(1),
        name="ada",
    )(cc, w, b.reshape(1, n))


def _ffn_kernel(*refs, mod_idx, ln_idx, mix, tm):
    if mix is None:
        x_ref, mod_ref, wg_ref, wu_ref, wd_ref, lng_ref, lnb_ref, o_ref, a_scr = refs
    else:
        (x_ref, ret_ref, lru_ref, wo_ref, mod_ref, wg_ref, wu_ref, wd_ref, lng_ref, lnb_ref,
         o_ref, a_scr) = refs
    m = mod_ref[0]
    i_shift, i_scale, i_gate = mod_idx
    sub = min(tm, FFN_SUB_TILE)
    for st in range(tm // sub):
        rows = slice(st * sub, (st + 1) * sub)
        h = x_ref[0, rows, :]
        if mix is not None:
            gate_idx, mix_ln_idx = mix
            z = jnp.concatenate([ret_ref[0, rows, :], lru_ref[0, rows, :]], -1)
            y = jnp.dot(z, wo_ref[...], preferred_element_type=F32)
            h = _layer_norm(ALPHA * h + m[gate_idx:gate_idx + 1] * y,
                            lng_ref[mix_ln_idx:mix_ln_idx + 1], lnb_ref[mix_ln_idx:mix_ln_idx + 1])
        u = (h * (1.0 + m[i_scale:i_scale + 1]) + m[i_shift:i_shift + 1]).astype(BF16)
        for j in range(N_FF_CHUNKS):
            cols = slice(j * FF_CHUNK, (j + 1) * FF_CHUNK)
            g = jnp.dot(u, wg_ref[:, cols], preferred_element_type=F32)
            p = jnp.dot(u, wu_ref[:, cols], preferred_element_type=F32)
            a_scr[rows, cols] = (g * _sigmoid(g) * p).astype(BF16)
        f = jnp.dot(a_scr[rows, :], wd_ref[...], preferred_element_type=F32)
        y = ALPHA * h + (MACARON * m[i_gate:i_gate + 1]) * f
        o_ref[0, rows, :] = _layer_norm(y, lng_ref[ln_idx:ln_idx + 1], lnb_ref[ln_idx:ln_idx + 1])


def _ffn(x, mods, wg, wu, wd, ln_g, ln_b, *, mod_idx, ln_idx, mix=None, mix_in=None):
    bsz, t, d = x.shape
    tm = min(TOKEN_TILE, t)
    tok = lambda w: pl.BlockSpec((1, tm, w), lambda b, i: (b, i, 0))
    in_specs = [tok(d)]
    args = [x]
    if mix is not None:
        ret_out, lru_out, w_out = mix_in
        in_specs += [tok(RET_WIDTH), tok(LRU_WIDTH), _const_spec(w_out.shape)]
        args += [ret_out, lru_out, w_out]
    in_specs += [pl.BlockSpec((1, 3 * N_SUB, d), lambda b, i: (b, 0, 0)),
                 _const_spec(wg.shape), _const_spec(wu.shape), _const_spec(wd.shape),
                 _const_spec(ln_g.shape), _const_spec(ln_b.shape)]
    args += [mods, wg, wu, wd, ln_g, ln_b]
    return pl.pallas_call(
        functools.partial(_ffn_kernel, mod_idx=mod_idx, ln_idx=ln_idx, mix=mix, tm=tm),
        out_shape=jax.ShapeDtypeStruct((bsz, t, d), F32),
        grid=(bsz, t // tm),
        in_specs=in_specs,
        out_specs=tok(d),
        scratch_shapes=[pltpu.VMEM((tm, D_FF), BF16)],
        compiler_params=_params(2),
        name="ffn_mix" if mix is not None else "ffn",
    )(*args)


def _lru_conv_slab(sl, xl, before, after, cw_ref, cb_ref, xe_scr, tm):
    seg = tm // SUBLANES
    pre = LRU_PRE
    ln = slice(sl * LANES, (sl + 1) * LANES)
    row1 = lax.broadcasted_iota(jnp.int32, (SUBLANES, LANES), 0)
    for s in range(SUBLANES):
        xe_scr[sl, pl.ds(pre + s, seg, stride=SUBLANES), :] = xl[s * seg:(s + 1) * seg, ln]
    for d in range(CONV_PAD_LO):
        src = pre + (seg - 1 - d) * SUBLANES
        edge = jnp.broadcast_to(before[HALO - 1 - d:HALO - d, ln], (SUBLANES, LANES))
        xe_scr[sl, pre - (d + 1) * SUBLANES:pre - d * SUBLANES, :] = jnp.where(
            row1 == 0, edge, pltpu.roll(xe_scr[sl, src:src + SUBLANES, :], 1, 0))
    edge = jnp.broadcast_to(after[0:1, ln], (SUBLANES, LANES))
    xe_scr[sl, pre + tm:pre + tm + SUBLANES, :] = jnp.where(
        row1 == SUBLANES - 1, edge, pltpu.roll(xe_scr[sl, pre:pre + SUBLANES, :], SUBLANES - 1, 0))
    acc = cb_ref[:, ln]
    for k in range(CONV_W):
        off = pre + (k - CONV_PAD_LO) * SUBLANES
        acc = acc + xe_scr[sl, off:off + tm, :] * cw_ref[k:k + 1, ln]
    return acc


def _lru_gates_slab(xc, w_gates):
    return jnp.dot(xc.astype(BF16), w_gates, preferred_element_type=F32)


def _lru_coeffs_slab(sl, xc, gi, bias, lam, a_scr, b_scr):
    th = jnp.tanh(gi + bias)
    th_r = th[:, :LANES]
    th_i = th[:, LANES:]
    c = (-0.5 * LRU_C) * _softplus(-lam)
    log_a = c * th_r + c
    a = jnp.exp(log_a)
    a_scr[sl] = a
    t4 = (-0.25 * jnp.tanh(log_a)) * (1.0 + a * a)
    half_root = jnp.where(t4 > 0.0, t4 * lax.rsqrt(t4), 0.0)
    b_scr[sl] = (half_root * xc) * (th_i + 1.0)


def _sublane_scan(a, b, hprev, reverse):
    row = lax.broadcasted_iota(jnp.int32, a.shape, 0)
    for k in (1, 2, 4):
        if reverse:
            sh = SUBLANES - k
            msk = row < sh
        else:
            sh = k
            msk = row >= k
        a_s = pltpu.roll(a, sh, 0)
        b_s = pltpu.roll(b, sh, 0)
        b = jnp.where(msk, a * b_s + b, b)
        a = jnp.where(msk, a * a_s, a)
    return a * hprev + b


def _lru_scan_slab(sl, a_scr, b_scr, carry_in, reverse, tm):
    seg = tm // SUBLANES
    row = lax.broadcasted_iota(jnp.int32, (SUBLANES, LANES), 0)
    hh = jnp.zeros((SUBLANES, LANES), F32)
    pp = jnp.ones((SUBLANES, LANES), F32)
    for jj in range(seg):
        j = seg - 1 - jj if reverse else jj
        rows = slice(j * SUBLANES, (j + 1) * SUBLANES)
        aj = a_scr[sl, rows, :]
        hh = aj * hh + b_scr[sl, rows, :]
        pp = aj * pp
        b_scr[sl, rows, :] = hh
        a_scr[sl, rows, :] = pp
    ends = _sublane_scan(pp, hh, carry_in, reverse)
    if reverse:
        enter = jnp.where(row == SUBLANES - 1, carry_in, pltpu.roll(ends, SUBLANES - 1, 0))
        leave = ends[0:1]
    else:
        enter = jnp.where(row == 0, carry_in, pltpu.roll(ends, 1, 0))
        leave = ends[SUBLANES - 1:SUBLANES]
    h = (b_scr[sl].reshape(seg, SUBLANES, LANES)
         + a_scr[sl].reshape(seg, SUBLANES, LANES) * enter[None]).reshape(tm, LANES)
    return h, leave


def _lru_scratch(tm):
    nl = LRU_WIDTH // LANES
    return [pltpu.VMEM((nl, tm + LRU_EXT, LANES), F32),
            pltpu.VMEM((nl, tm, LANES), F32),
            pltpu.VMEM((nl, tm, LANES), F32),
            pltpu.VMEM((SUBLANES, LRU_WIDTH), F32)]


def _inproj_kernel(x_ref, mod_ref, w_ref, cos_ref, sa_ref, sb_ref,
                   cw_ref, cb_ref, wg_ref, bias_ref, lam_ref, h0_ref,
                   k_ref, v_ref, q_ref, g_ref, gate_ref, xc_ref, hf_ref, hfl_ref,
                   xe_scr, a_scr, b_scr, carry_scr, xl_scr, tail_scr, *, tm, nt):
    s = pl.program_id(0)
    ip = (jnp.maximum(s, 1) - 1) % nt

    @pl.when(s == 0)
    def _():
        xl_scr[...] = jnp.zeros_like(xl_scr)
        tail_scr[...] = jnp.zeros_like(tail_scr)
        carry_scr[...] = jnp.zeros_like(carry_scr)

    m = mod_ref[0]
    u = (x_ref[0] * (1.0 + m[4:5]) + m[3:4]).astype(BF16)
    cos = cos_ref[...]
    sa = sa_ref[...]
    sb = sb_ref[...]

    def seg(j):
        return jnp.dot(u, w_ref[:, j * SEG:(j + 1) * SEG], preferred_element_type=F32)

    def rope(t):
        outs = []
        for hd in range(RET_HEADS):
            th = t[:, hd * RET_DK:(hd + 1) * RET_DK]
            outs.append(th * cos + pltpu.roll(th, RET_DK - 32, 1) * sa + pltpu.roll(th, 32, 1) * sb)
        return jnp.concatenate(outs, -1)

    xl_new = seg(X_SEG)

    nl = LRU_WIDTH // LANES
    xl_old = xl_scr[...]
    before = jnp.where(ip > 0, tail_scr[...], 0.0)
    after = jnp.where(ip < nt - 1, xl_new[0:HALO], 0.0)
    carry_in = jnp.where(ip == 0, jnp.broadcast_to(h0_ref[0, 0:1], carry_scr.shape), carry_scr[...])

    xc = [_lru_conv_slab(sl, xl_old, before, after, cw_ref, cb_ref, xe_scr, tm) for sl in range(nl)]
    for sl in range(nl):
        xc_ref[0, :, sl * LANES:(sl + 1) * LANES] = xc[sl]
    tail_scr[...] = xl_old[tm - HALO:tm]
    xl_scr[...] = xl_new
    gi = [_lru_gates_slab(xc[sl], wg_ref[0, sl]) for sl in range(nl)]

    k_ref[0] = (rope(seg(K_SEG)) * K_SCALE).astype(BF16)
    v_ref[0] = seg(V_SEG).astype(BF16)
    q_ref[0] = rope(seg(Q_SEG)).astype(BF16)
    g_ref[0] = seg(G_SEG)
    gate_ref[0] = seg(GATE_SEG)

    leaves = []
    for sl in range(nl):
        ln = slice(sl * LANES, (sl + 1) * LANES)
        _lru_coeffs_slab(sl, xc[sl], gi[sl], bias_ref[0, sl], lam_ref[0, :, ln], a_scr, b_scr)
        h, leave = _lru_scan_slab(sl, a_scr, b_scr, carry_in[:, ln], False, tm)
        hf_ref[0, :, ln] = h
        leaves.append(leave)
    leave = jnp.concatenate(leaves, -1)
    carry_scr[...] = jnp.broadcast_to(leave, carry_scr.shape)

    @pl.when(ip == nt - 1)
    def _():
        hfl_ref[0] = leave


def _inproj(x, mods, w_in, tables, conv_w, conv_b, w_gates, bias, lam, h0):
    bsz, t, d = x.shape
    tm = min(TOKEN_TILE, t)
    nt = t // tm
    ntiles = bsz * nt
    cur = lambda s: jnp.minimum(s, ntiles - 1)
    old = lambda s: jnp.maximum(s, 1) - 1
    tok = lambda w: pl.BlockSpec((1, tm, w), lambda s: (cur(s) // nt, cur(s) % nt, 0))
    tab = pl.BlockSpec((tm, RET_DK), lambda s: (cur(s) % nt, 0))
    shp = lambda dt: jax.ShapeDtypeStruct((bsz, t, SEG), dt)
    return pl.pallas_call(
        functools.partial(_inproj_kernel, tm=tm, nt=nt),
        out_shape=[shp(BF16), shp(BF16), shp(BF16), shp(F32), shp(F32), shp(F32), shp(F32),
                   jax.ShapeDtypeStruct((bsz, 1, LRU_WIDTH), F32)],
        grid=(ntiles + 1,),
        in_specs=[tok(d), pl.BlockSpec((1, 3 * N_SUB, d), lambda s: (cur(s) // nt, 0, 0)),
                  _const_spec(w_in.shape), tab, tab, tab,
                  _const_spec(conv_w.shape), _const_spec(conv_b.shape), _const_spec(w_gates.shape),
                  _const_spec(bias.shape), _const_spec(lam.shape),
                  pl.BlockSpec((1, 2, LRU_WIDTH), lambda s: (old(s) // nt, 0, 0))],
        out_specs=[tok(SEG)] * 5
        + [pl.BlockSpec((1, tm, LRU_WIDTH), lambda s: (old(s) // nt, old(s) % nt, 0))] * 2
        + [pl.BlockSpec((1, 1, LRU_WIDTH), lambda s: (old(s) // nt, 0, 0))],
        scratch_shapes=_lru_scratch(tm) + [pltpu.VMEM((tm, LRU_WIDTH), F32),
                                           pltpu.VMEM((HALO, LRU_WIDTH), F32)],
        compiler_params=_params(1),
        name="inproj",
    )(x, mods, w_in, *tables, conv_w, conv_b, w_gates, bias, lam, h0)


def _lru_bwd_kernel(xc_ref, gate_ref, hf_ref, wg_ref, bias_ref, lam_ref, h0_ref,
                    o_ref, hbf_ref, a_scr, b_scr, carry_scr, out_scr, *, tm, nt):
    i = pl.program_id(1)
    nl = LRU_WIDTH // LANES
    seg = tm // SUBLANES

    @pl.when(i == 0)
    def _():
        carry_scr[...] = jnp.broadcast_to(h0_ref[0, 1:2], carry_scr.shape)

    carry_in = carry_scr[...]
    leaves = []
    for sl in range(nl):
        ln = slice(sl * LANES, (sl + 1) * LANES)
        xc = xc_ref[0, :, ln]
        gi = _lru_gates_slab(xc, wg_ref[1, sl])
        _lru_coeffs_slab(sl, xc, gi, bias_ref[1, sl], lam_ref[1, :, ln], a_scr, b_scr)
        h, leave = _lru_scan_slab(sl, a_scr, b_scr, carry_in[:, ln], True, tm)
        out_scr[sl] = hf_ref[0, :, ln] + h
        leaves.append(leave)
    leave = jnp.concatenate(leaves, -1)
    carry_scr[...] = jnp.broadcast_to(leave, carry_scr.shape)

    @pl.when(i == nt - 1)
    def _():
        hbf_ref[0] = leave

    for s in range(SUBLANES):
        ht = jnp.concatenate([out_scr[sl, pl.ds(s, seg, stride=SUBLANES), :]
                              for sl in range(nl)], -1)
        rows = slice(s * seg, (s + 1) * seg)
        o_ref[0, rows, :] = (ht * jax.nn.gelu(gate_ref[0, rows, :])).astype(BF16)


def _lru_bwd(xc, gate, hf, w_gates, bias, lam, h0):
    bsz, t, w = xc.shape
    tm = min(TOKEN_TILE, t)
    nt = t // tm
    tok = pl.BlockSpec((1, tm, w), lambda b, i: (b, nt - 1 - i, 0))
    state = pl.BlockSpec((1, 1, w), lambda b, i: (b, 0, 0))
    return pl.pallas_call(
        functools.partial(_lru_bwd_kernel, tm=tm, nt=nt),
        out_shape=[jax.ShapeDtypeStruct((bsz, t, w), BF16),
                   jax.ShapeDtypeStruct((bsz, 1, w), F32)],
        grid=(bsz, nt),
        in_specs=[tok, tok, tok, _const_spec(w_gates.shape), _const_spec(bias.shape),
                  _const_spec(lam.shape), pl.BlockSpec((1, 2, w), lambda b, i: (b, 0, 0))],
        out_specs=[tok, state],
        scratch_shapes=_lru_scratch(tm)[1:] + [pltpu.VMEM((w // LANES, tm, LANES), F32)],
        compiler_params=_params(2),
        name="lru_bwd",
    )(xc, gate, hf, w_gates, bias, lam, h0)


def _ret_kernel(logit_ref, q_ref, k_ref, v_ref, g_ref, ng_ref, nb_ref, s0f_ref, s0b_ref,
                o_ref, sff_ref, sbf_ref, dmat_scr, kv_scr, ss_scr, *, nchunks):
    hp = pl.program_id(1)
    c = RET_BLOCK
    dk = RET_DK
    contract_cols = (((1,), (1,)), ((), ()))

    def rows(n):
        return pl.ds(pl.multiple_of(n * c, c), c)

    def log_decay(direction, hd, shape):
        z = jnp.full(shape, logit_ref[direction, hd], F32)
        return -_softplus(-z)

    rowi = lax.broadcasted_iota(jnp.int32, (c, c), 0)
    coli = lax.broadcasted_iota(jnp.int32, (c, c), 1)
    diff = (rowi - coli).astype(F32)
    pos = lax.broadcasted_iota(jnp.int32, (c, dk), 0).astype(F32)
    lpos = lax.broadcasted_iota(jnp.int32, (dk, c), 1).astype(F32)

    heads = []
    for hh in range(RET_HEADS_PER_STEP):
        hd = hp * RET_HEADS_PER_STEP + hh
        lgf = log_decay(0, hd, (c, c))
        lgb = log_decay(1, hd, (c, c))
        dmat_scr[hh] = jnp.where(diff >= 0, jnp.exp(lgf * jnp.maximum(diff, 0.0)),
                                 jnp.exp(lgb * jnp.maximum(-diff, 0.0)))
        lgf_r, lgb_r = log_decay(0, hd, (c, dk)), log_decay(1, hd, (c, dk))
        heads.append(dict(
            lanes=slice(hh * dk, (hh + 1) * dk),
            qdec_f=jnp.exp(lgf_r * (pos + 1.0)), qdec_b=jnp.exp(lgb_r * (c - pos)),
            kdec_f=jnp.exp(log_decay(0, hd, (dk, c)) * (c - 1.0 - lpos)),
            kdec_b=jnp.exp(log_decay(1, hd, (dk, c)) * lpos),
            gc_f=jnp.exp(log_decay(0, hd, (dk, dk)) * float(c)),
            gc_b=jnp.exp(log_decay(1, hd, (dk, dk)) * float(c))))

    def pass_a(n, carry):
        for hh, hv in enumerate(heads):
            kt = k_ref[0, rows(n), hv["lanes"]].astype(F32).T
            lhs = jnp.concatenate([kt * hv["kdec_f"], kt * hv["kdec_b"]], 0).astype(BF16)
            kv_scr[hh, n] = jnp.dot(lhs, v_ref[0, rows(n), hv["lanes"]],
                                    preferred_element_type=F32)
        return carry

    lax.fori_loop(0, nchunks, pass_a, 0, unroll=2 if nchunks > 1 else 1)

    for hh, hv in enumerate(heads):
        def pass_s(n, carry, hh=hh, hv=hv):
            sf, sb = carry
            m = nchunks - 1 - n
            ss_scr[hh, n, 0:dk] = sf.astype(BF16)
            ss_scr[hh, m, dk:2 * dk] = sb.astype(BF16)
            return (hv["gc_f"] * sf + kv_scr[hh, n, 0:dk],
                    hv["gc_b"] * sb + kv_scr[hh, m, dk:2 * dk])

        sf, sb = lax.fori_loop(0, nchunks, pass_s, (s0f_ref[0, hh], s0b_ref[0, hh]))
        sff_ref[0, hh] = sf
        sbf_ref[0, hh] = sb

    gn = ng_ref[...]
    bn = nb_ref[...]

    def pass_b(n, carry):
        for hh, hv in enumerate(heads):
            ln = hv["lanes"]
            qn = q_ref[0, rows(n), ln]
            sc = lax.dot_general(qn, k_ref[0, rows(n), ln], contract_cols,
                                 preferred_element_type=F32)
            o = jnp.dot((sc * dmat_scr[hh]).astype(BF16), v_ref[0, rows(n), ln],
                        preferred_element_type=F32)
            qf = qn.astype(F32)
            qd = jnp.concatenate([(qf * hv["qdec_f"]).astype(BF16),
                                  (qf * hv["qdec_b"]).astype(BF16)], -1)
            o = o + jnp.dot(qd, ss_scr[hh, n], preferred_element_type=F32)
            mu = jnp.mean(o, -1, keepdims=True)
            oc = o - mu
            var = jnp.mean(oc * oc, -1, keepdims=True)
            on = oc * lax.rsqrt(var + LN_EPS) * gn[:, ln] + bn[:, ln]
            gg = g_ref[0, rows(n), ln]
            o_ref[0, rows(n), ln] = (on * (gg * _sigmoid(gg))).astype(BF16)
        return carry

    lax.fori_loop(0, nchunks, pass_b, 0, unroll=2 if nchunks > 1 else 1)


def _ret(logit, q, k, v, g, norm_g, norm_b, s0f, s0b):
    bsz, t, _ = q.shape
    nchunks = t // RET_BLOCK
    hps = RET_HEADS_PER_STEP
    wid = hps * RET_DK
    head = pl.BlockSpec((1, t, wid), lambda b, h: (b, 0, h))
    vec = pl.BlockSpec((1, wid), lambda b, h: (0, h))
    st = pl.BlockSpec((1, hps, RET_DK, RET_DK), lambda b, h: (b, h, 0, 0))
    st_shape = jax.ShapeDtypeStruct((bsz, RET_HEADS, RET_DK, RET_DK), F32)
    return pl.pallas_call(
        functools.partial(_ret_kernel, nchunks=nchunks),
        out_shape=[jax.ShapeDtypeStruct((bsz, t, RET_WIDTH), BF16), st_shape, st_shape],
        grid=(bsz, RET_HEADS // hps),
        in_specs=[pl.BlockSpec(memory_space=pltpu.SMEM), head, head, head, head, vec, vec, st, st],
        out_specs=[head, st, st],
        scratch_shapes=[pltpu.VMEM((hps, RET_BLOCK, RET_BLOCK), F32),
                        pltpu.VMEM((hps, nchunks, 2 * RET_DK, RET_DK), F32),
                        pltpu.VMEM((hps, nchunks, 2 * RET_DK, RET_DK), BF16)],
        compiler_params=_params(2),
        name="ret",
    )(logit, q, k, v, g, norm_g, norm_b, s0f, s0b)


def _rope_tables(t):
    n = RET_DK // 4
    inv = ROPE_BASE ** (-jnp.arange(n, dtype=F32) / n)
    pos = jnp.arange(t)
    a_r = (pos // GRID_W).astype(F32)[:, None] * inv
    a_c = (pos % GRID_W).astype(F32)[:, None] * inv
    ang = jnp.concatenate([a_r, a_r, a_c, a_c], -1)
    first = (jnp.arange(RET_DK) % (2 * n)) < n
    sin = jnp.sin(ang)
    return jnp.cos(ang), jnp.where(first, -sin, 0.0), jnp.where(first, 0.0, sin)


def _block_diag(wb):
    nb, c, d = wb.shape
    eye = jnp.eye(nb, dtype=wb.dtype)
    return jnp.einsum('ncd,nm->ncmd', wb, eye).reshape(nb * c, nb * d)


def kernel(x, c, ctx, c_ctx, w_ada, b_ada, ffn1_w_gate, ffn1_w_up, ffn1_w_down, ffn2_w_gate,
           ffn2_w_up, ffn2_w_down, w_in, w_out, ret_decay_logit, ret_norm_g, ret_norm_b,
           lru_conv_w, lru_conv_b, lru_w_a, lru_b_a, lru_w_i, lru_b_i, lru_lambda, ln_g, ln_b):
    bsz, t, d = x.shape
    tc = ctx.shape[1]
    l = 0

    pad = jnp.zeros((2 * SUBLANES - bsz - 1, d), F32)
    m = _ada(jnp.concatenate([c, c_ctx[None], pad], 0), w_ada[l], b_ada[l])
    mods_lat = m[:bsz].reshape(bsz, 3 * N_SUB, d)
    mods_ctx = jnp.broadcast_to(m[bsz].reshape(1, 3 * N_SUB, d), (bsz, 3 * N_SUB, d))

    ffn1_w = (ffn1_w_gate[l].astype(BF16), ffn1_w_up[l].astype(BF16), ffn1_w_down[l].astype(BF16))
    ffn2_w = (ffn2_w_gate[l].astype(BF16), ffn2_w_up[l].astype(BF16), ffn2_w_down[l].astype(BF16))
    w_in_b = w_in[l].astype(BF16)
    w_out_b = w_out[l].astype(BF16)
    lng, lnb = ln_g[l], ln_b[l]

    nl = LRU_WIDTH // LANES
    per_slab = LRU_BLOCKS // nl

    def slab_blocks(wb):
        wb = wb.reshape(2 * nl, per_slab, wb.shape[-2], wb.shape[-1])
        return jax.vmap(_block_diag)(wb).reshape(2, nl, LANES, LANES)

    w_gates = (0.5 * jnp.concatenate([slab_blocks(lru_w_a[l]), slab_blocks(lru_w_i[l])], -1)
               ).astype(BF16)
    gate_bias = 0.5 * jnp.concatenate([lru_b_a[l].reshape(2, nl, 1, LANES),
                                       lru_b_i[l].reshape(2, nl, 1, LANES)], -1)
    lam = lru_lambda[l][:, None, :]
    lru_w = dict(w_gates=w_gates, bias=gate_bias, lam=lam)
    conv_w = dict(conv_w=lru_conv_w[l], conv_b=lru_conv_b[l][None, :])
    norm_g = ret_norm_g[l][None, :]
    norm_b = ret_norm_b[l][None, :]
    logit = ret_decay_logit[l]

    tables_lat = _rope_tables(t)
    tables_ctx = (jnp.ones((tc, RET_DK), F32), jnp.zeros((tc, RET_DK), F32),
                  jnp.zeros((tc, RET_DK), F32))

    hc = _ffn(ctx, mods_ctx, *ffn1_w, lng, lnb, mod_idx=(0, 1, 2), ln_idx=0)
    h0_zero = jnp.zeros((bsz, 2, LRU_WIDTH), F32)
    kc, vc, qc, gc, gatec, xcc, hfc, hcf_last = _inproj(hc, mods_ctx, w_in_b, tables_ctx,
                                                        h0=h0_zero, **conv_w, **lru_w)
    _, hcb_first = _lru_bwd(xcc, gatec, hfc, h0=h0_zero, **lru_w)
    s_zero = jnp.zeros((bsz, RET_HEADS, RET_DK, RET_DK), F32)
    _, s_cf, s_cb = _ret(logit, qc, kc, vc, gc, norm_g, norm_b, s_zero, s_zero)

    h0 = jnp.concatenate([hcf_last, hcb_first], 1)
    x1 = _ffn(x, mods_lat, *ffn1_w, lng, lnb, mod_idx=(0, 1, 2), ln_idx=0)
    k, v, q, g, gate, xc, hf, _ = _inproj(x1, mods_lat, w_in_b, tables_lat, h0=h0,
                                          **conv_w, **lru_w)
    lru_out, _ = _lru_bwd(xc, gate, hf, h0=h0, **lru_w)
    ret_out, _, _ = _ret(logit, q, k, v, g, norm_g, norm_b, s_cf, s_cb)
    return _ffn(x1, mods_lat, *ffn2_w, lng, lnb, mod_idx=(6, 7, 8), ln_idx=2,
                mix=(5, 1), mix_in=(ret_out, lru_out, w_out_b))
```

```python
import functools

import jax
import jax.numpy as jnp
import numpy as np
from jax import lax
from jax.experimental import pallas as pl
from jax.experimental.pallas import tpu as pltpu

F32 = jnp.float32
BF16 = jnp.bfloat16

D_MODEL = 1024
D_FF = 2816
N_SUB = 3
RET_WIDTH = 512
RET_HEADS = 4
RET_DK = RET_WIDTH // RET_HEADS
RET_BLOCK = 256
RET_HEADS_PER_STEP = 2
RET_UNROLL = 4
LRU_WIDTH = 512
LRU_BLOCKS = 8
LRU_C = 8.0
CONV_W = 4
CONV_PAD_LO = 2
GRID_W = 64
DEPTH = 1
MACARON = 0.5
ALPHA = (2.0 * DEPTH) ** 0.25
ROPE_BASE = 10000.0
LN_EPS = 1e-5
K_SCALE = RET_DK ** -0.5

SEG = 512
K_SEG, V_SEG, X_SEG, Q_SEG, G_SEG, GATE_SEG = range(6)

LANES = 128
SUBLANES = 8
BF16_ROWS = 16
MXU_DIM = 256
VMEM_LIMIT_BYTES = 56 * 1024 * 1024

FF_CHUNK = MXU_DIM
N_FF_CHUNKS = D_FF // FF_CHUNK
TOKEN_TILE = 512
FFN_SUB_TILE = 512
ADA_COL_TILE = 1024
HALO = SUBLANES
LRU_PRE = CONV_PAD_LO * SUBLANES
LRU_EXT = (CONV_W - 1) * SUBLANES


def _layer_norm(y, g, b):
    mu = jnp.mean(y, -1, keepdims=True)
    yc = y - mu
    var = jnp.mean(yc * yc, -1, keepdims=True)
    return yc * lax.rsqrt(var + LN_EPS) * g + b


def _sigmoid(x):
    return 0.5 * jnp.tanh(0.5 * x) + 0.5


def _softplus(z):
    return jnp.maximum(z, 0.0) + jnp.log(1.0 + jnp.exp(-jnp.abs(z)))


def _const_spec(shape):
    zeros = (0,) * len(shape)
    return pl.BlockSpec(shape, lambda *_: zeros, pipeline_mode=pl.Buffered(1))


def _params(n_grid_axes):
    return pltpu.CompilerParams(dimension_semantics=("arbitrary",) * n_grid_axes,
                                vmem_limit_bytes=VMEM_LIMIT_BYTES)


def _ada_kernel(c_ref, w_ref, b_ref, o_ref):
    c = c_ref[...]
    s = c * _sigmoid(c)
    o_ref[...] = jnp.dot(s, w_ref[...], preferred_element_type=F32) + b_ref[...]


def _ada(cc, w, b):
    rows, d = cc.shape
    n = w.shape[1]
    tn = ADA_COL_TILE
    return pl.pallas_call(
        _ada_kernel,
        out_shape=jax.ShapeDtypeStruct((rows, n), F32),
        grid=(n // tn,),
        in_specs=[pl.BlockSpec((rows, d), lambda j: (0, 0)),
                  pl.BlockSpec((d, tn), lambda j: (0, j)),
                  pl.BlockSpec((1, tn), lambda j: (0, j))],
        out_specs=pl.BlockSpec((rows, tn), lambda j: (0, j)),
        compiler_params=_params(1),
        name="ada",
    )(cc, w, b.reshape(1, n))


def _ffn_kernel(*refs, mod_idx, ln_idx, mix, tm):
    if mix is None:
        x_ref, mod_ref, wg_ref, wu_ref, wd_ref, lng_ref, lnb_ref, o_ref, a_scr = refs
    else:
        (x_ref, ret_ref, lru_ref, wo_ref, mod_ref, wg_ref, wu_ref, wd_ref, lng_ref, lnb_ref,
         o_ref, a_scr) = refs
    m = mod_ref[0]
    i_shift, i_scale, i_gate = mod_idx
    sub = min(tm, FFN_SUB_TILE)
    for st in range(tm // sub):
        rows = slice(st * sub, (st + 1) * sub)
        h = x_ref[0, rows, :]
        if mix is not None:
            gate_idx, mix_ln_idx = mix
            z = jnp.concatenate([ret_ref[0, rows, :], lru_ref[0, rows, :]], -1)
            y = jnp.dot(z, wo_ref[...], preferred_element_type=F32)
            h = _layer_norm(ALPHA * h + m[gate_idx:gate_idx + 1] * y,
                            lng_ref[mix_ln_idx:mix_ln_idx + 1], lnb_ref[mix_ln_idx:mix_ln_idx + 1])
        u = (h * (1.0 + m[i_scale:i_scale + 1]) + m[i_shift:i_shift + 1]).astype(BF16)
        for j in range(N_FF_CHUNKS):
            cols = slice(j * FF_CHUNK, (j + 1) * FF_CHUNK)
            g = jnp.dot(u, wg_ref[:, cols], preferred_element_type=F32)
            p = jnp.dot(u, wu_ref[:, cols], preferred_element_type=F32)
            a_scr[rows, cols] = (g * _sigmoid(g) * p).astype(BF16)
        f = jnp.dot(a_scr[rows, :], wd_ref[...], preferred_element_type=F32)
        y = ALPHA * h + (MACARON * m[i_gate:i_gate + 1]) * f
        o_ref[0, rows, :] = _layer_norm(y, lng_ref[ln_idx:ln_idx + 1], lnb_ref[ln_idx:ln_idx + 1])


def _ffn(x, mods, wg, wu, wd, ln_g, ln_b, *, mod_idx, ln_idx, mix=None, mix_in=None):
    bsz, t, d = x.shape
    tm = min(TOKEN_TILE, t)
    tok = lambda w: pl.BlockSpec((1, tm, w), lambda b, i: (b, i, 0))
    in_specs = [tok(d)]
    args = [x]
    if mix is not None:
        ret_out, lru_out, w_out = mix_in
        in_specs += [tok(RET_WIDTH), tok(LRU_WIDTH), _const_spec(w_out.shape)]
        args += [ret_out, lru_out, w_out]
    in_specs += [pl.BlockSpec((1, 3 * N_SUB, d), lambda b, i: (b, 0, 0)),
                 _const_spec(wg.shape), _const_spec(wu.shape), _const_spec(wd.shape),
                 _const_spec(ln_g.shape), _const_spec(ln_b.shape)]
    args += [mods, wg, wu, wd, ln_g, ln_b]
    return pl.pallas_call(
        functools.partial(_ffn_kernel, mod_idx=mod_idx, ln_idx=ln_idx, mix=mix, tm=tm),
        out_shape=jax.ShapeDtypeStruct((bsz, t, d), F32),
        grid=(bsz, t // tm),
        in_specs=in_specs,
        out_specs=tok(d),
        scratch_shapes=[pltpu.VMEM((tm, D_FF), BF16)],
        compiler_params=_params(2),
        name="ffn_mix" if mix is not None else "ffn",
    )(*args)


def _lru_conv_slab(sl, xl, before, after, cw_ref, cb_ref, xe_scr, tm):
    seg = tm // SUBLANES
    pre = LRU_PRE
    ln = slice(sl * LANES, (sl + 1) * LANES)
    row1 = lax.broadcasted_iota(jnp.int32, (SUBLANES, LANES), 0)
    for s in range(SUBLANES):
        xe_scr[sl, pl.ds(pre + s, seg, stride=SUBLANES), :] = xl[s * seg:(s + 1) * seg, ln]
    for d in range(CONV_PAD_LO):
        src = pre + (seg - 1 - d) * SUBLANES
        edge = jnp.broadcast_to(before[HALO - 1 - d:HALO - d, ln], (SUBLANES, LANES))
        xe_scr[sl, pre - (d + 1) * SUBLANES:pre - d * SUBLANES, :] = jnp.where(
            row1 == 0, edge, pltpu.roll(xe_scr[sl, src:src + SUBLANES, :], 1, 0))
    edge = jnp.broadcast_to(after[0:1, ln], (SUBLANES, LANES))
    xe_scr[sl, pre + tm:pre + tm + SUBLANES, :] = jnp.where(
        row1 == SUBLANES - 1, edge, pltpu.roll(xe_scr[sl, pre:pre + SUBLANES, :], SUBLANES - 1, 0))
    acc = cb_ref[:, ln]
    for k in range(CONV_W):
        off = pre + (k - CONV_PAD_LO) * SUBLANES
        acc = acc + xe_scr[sl, off:off + tm, :] * cw_ref[k:k + 1, ln]
    return acc


def _lru_gates_slab(xc, w_gates):
    return jnp.dot(xc.astype(BF16), w_gates, preferred_element_type=F32)


def _lru_coeffs_slab(sl, xc, gi, bias, lam, a_scr, b_scr):
    th = jnp.tanh(gi + bias)
    th_r = th[:, :LANES]
    th_i = th[:, LANES:]
    c = (-0.5 * LRU_C) * _softplus(-lam)
    log_a = c * th_r + c
    a = jnp.exp(log_a)
    a_scr[sl] = a
    t4 = (-0.25 * jnp.tanh(log_a)) * (1.0 + a * a)
    half_root = jnp.where(t4 > 0.0, t4 * lax.rsqrt(t4), 0.0)
    b_scr[sl] = (half_root * xc) * (th_i + 1.0)


def _sublane_scan(a, b, hprev, reverse):
    row = lax.broadcasted_iota(jnp.int32, a.shape, 0)
    for k in (1, 2, 4):
        if reverse:
            sh = SUBLANES - k
            msk = row < sh
        else:
            sh = k
            msk = row >= k
        a_s = pltpu.roll(a, sh, 0)
        b_s = pltpu.roll(b, sh, 0)
        b = jnp.where(msk, a * b_s + b, b)
        a = jnp.where(msk, a * a_s, a)
    return a * hprev + b


def _lru_scan_slab(sl, a_scr, b_scr, carry_in, reverse, tm):
    seg = tm // SUBLANES
    row = lax.broadcasted_iota(jnp.int32, (SUBLANES, LANES), 0)
    hh = jnp.zeros((SUBLANES, LANES), F32)
    pp = jnp.ones((SUBLANES, LANES), F32)
    for jj in range(seg):
        j = seg - 1 - jj if reverse else jj
        rows = slice(j * SUBLANES, (j + 1) * SUBLANES)
        aj = a_scr[sl, rows, :]
        hh = aj * hh + b_scr[sl, rows, :]
        pp = aj * pp
        b_scr[sl, rows, :] = hh
        a_scr[sl, rows, :] = pp
    ends = _sublane_scan(pp, hh, carry_in, reverse)
    if reverse:
        enter = jnp.where(row == SUBLANES - 1, carry_in, pltpu.roll(ends, SUBLANES - 1, 0))
        leave = ends[0:1]
    else:
        enter = jnp.where(row == 0, carry_in, pltpu.roll(ends, 1, 0))
        leave = ends[SUBLANES - 1:SUBLANES]
    h = (b_scr[sl].reshape(seg, SUBLANES, LANES)
         + a_scr[sl].reshape(seg, SUBLANES, LANES) * enter[None]).reshape(tm, LANES)
    return h, leave


def _lru_scratch(tm):
    nl = LRU_WIDTH // LANES
    return [pltpu.VMEM((nl, tm + LRU_EXT, LANES), F32),
            pltpu.VMEM((nl, tm, LANES), F32),
            pltpu.VMEM((nl, tm, LANES), F32),
            pltpu.VMEM((SUBLANES, LRU_WIDTH), F32)]


def _inproj_kernel(x_ref, mod_ref, w_ref, cos_ref, sa_ref, sb_ref,
                   cw_ref, cb_ref, wg_ref, bias_ref, lam_ref, h0_ref,
                   k_ref, v_ref, q_ref, g_ref, gate_ref, xc_ref, hf_ref, hfl_ref,
                   xe_scr, a_scr, b_scr, carry_scr, xl_scr, tail_scr, *, tm, nt):
    s = pl.program_id(0)
    ip = (jnp.maximum(s, 1) - 1) % nt

    @pl.when(s == 0)
    def _():
        xl_scr[...] = jnp.zeros_like(xl_scr)
        tail_scr[...] = jnp.zeros_like(tail_scr)
        carry_scr[...] = jnp.zeros_like(carry_scr)

    m = mod_ref[0]
    u = (x_ref[0] * (1.0 + m[4:5]) + m[3:4]).astype(BF16)
    cos = cos_ref[...]
    sa = sa_ref[...]
    sb = sb_ref[...]

    def seg(j):
        return jnp.dot(u, w_ref[:, j * SEG:(j + 1) * SEG], preferred_element_type=F32)

    def rope(t):
        outs = []
        for hd in range(RET_HEADS):
            th = t[:, hd * RET_DK:(hd + 1) * RET_DK]
            outs.append(th * cos + pltpu.roll(th, RET_DK - 32, 1) * sa + pltpu.roll(th, 32, 1) * sb)
        return jnp.concatenate(outs, -1)

    xl_new = seg(X_SEG)

    nl = LRU_WIDTH // LANES
    xl_old = xl_scr[...]
    before = jnp.where(ip > 0, tail_scr[...], 0.0)
    after = jnp.where(ip < nt - 1, xl_new[0:HALO], 0.0)
    carry_in = jnp.where(ip == 0, jnp.broadcast_to(h0_ref[0, 0:1], carry_scr.shape), carry_scr[...])

    xc = [_lru_conv_slab(sl, xl_old, before, after, cw_ref, cb_ref, xe_scr, tm) for sl in range(nl)]
    for sl in range(nl):
        xc_ref[0, :, sl * LANES:(sl + 1) * LANES] = xc[sl]
    tail_scr[...] = xl_old[tm - HALO:tm]
    xl_scr[...] = xl_new
    gi = [_lru_gates_slab(xc[sl], wg_ref[0, sl]) for sl in range(nl)]

    k_ref[0] = (rope(seg(K_SEG)) * K_SCALE).astype(BF16)
    v_ref[0] = seg(V_SEG).astype(BF16)
    q_ref[0] = rope(seg(Q_SEG)).astype(BF16)
    g_ref[0] = seg(G_SEG)
    gate_ref[0] = seg(GATE_SEG)

    leaves = []
    for sl in range(nl):
        ln = slice(sl * LANES, (sl + 1) * LANES)
        _lru_coeffs_slab(sl, xc[sl], gi[sl], bias_ref[0, sl], lam_ref[0, :, ln], a_scr, b_scr)
        h, leave = _lru_scan_slab(sl, a_scr, b_scr, carry_in[:, ln], False, tm)
        hf_ref[0, :, ln] = h
        leaves.append(leave)
    leave = jnp.concatenate(leaves, -1)
    carry_scr[...] = jnp.broadcast_to(leave, carry_scr.shape)

    @pl.when(ip == nt - 1)
    def _():
        hfl_ref[0] = leave


def _inproj(x, mods, w_in, tables, conv_w, conv_b, w_gates, bias, lam, h0):
    bsz, t, d = x.shape
    tm = min(TOKEN_TILE, t)
    nt = t // tm
    ntiles = bsz * nt
    cur = lambda s: jnp.minimum(s, ntiles - 1)
    old = lambda s: jnp.maximum(s, 1) - 1
    tok = lambda w: pl.BlockSpec((1, tm, w), lambda s: (cur(s) // nt, cur(s) % nt, 0))
    tab = pl.BlockSpec((tm, RET_DK), lambda s: (cur(s) % nt, 0))
    shp = lambda dt: jax.ShapeDtypeStruct((bsz, t, SEG), dt)
    return pl.pallas_call(
        functools.partial(_inproj_kernel, tm=tm, nt=nt),
        out_shape=[shp(BF16), shp(BF16), shp(BF16), shp(F32), shp(F32), shp(F32), shp(F32),
                   jax.ShapeDtypeStruct((bsz, 1, LRU_WIDTH), F32)],
        grid=(ntiles + 1,),
        in_specs=[tok(d), pl.BlockSpec((1, 3 * N_SUB, d), lambda s: (cur(s) // nt, 0, 0)),
                  _const_spec(w_in.shape), tab, tab, tab,
                  _const_spec(conv_w.shape), _const_spec(conv_b.shape), _const_spec(w_gates.shape),
                  _const_spec(bias.shape), _const_spec(lam.shape),
                  pl.BlockSpec((1, 2, LRU_WIDTH), lambda s: (old(s) // nt, 0, 0))],
        out_specs=[tok(SEG)] * 5
        + [pl.BlockSpec((1, tm, LRU_WIDTH), lambda s: (old(s) // nt, old(s) % nt, 0))] * 2
        + [pl.BlockSpec((1, 1, LRU_WIDTH), lambda s: (old(s) // nt, 0, 0))],
        scratch_shapes=_lru_scratch(tm) + [pltpu.VMEM((tm, LRU_WIDTH), F32),
                                           pltpu.VMEM((HALO, LRU_WIDTH), F32)],
        compiler_params=_params(1),
        name="inproj",
    )(x, mods, w_in, *tables, conv_w, conv_b, w_gates, bias, lam, h0)


def _lru_bwd_kernel(xc_ref, gate_ref, hf_ref, wg_ref, bias_ref, lam_ref, h0_ref,
                    o_ref, hbf_ref, a_scr, b_scr, carry_scr, out_scr, *, tm, nt):
    i = pl.program_id(1)
    nl = LRU_WIDTH // LANES
    seg = tm // SUBLANES

    @pl.when(i == 0)
    def _():
        carry_scr[...] = jnp.broadcast_to(h0_ref[0, 1:2], carry_scr.shape)

    carry_in = carry_scr[...]
    leaves = []
    for sl in range(nl):
        ln = slice(sl * LANES, (sl + 1) * LANES)
        xc = xc_ref[0, :, ln]
        gi = _lru_gates_slab(xc, wg_ref[1, sl])
        _lru_coeffs_slab(sl, xc, gi, bias_ref[1, sl], lam_ref[1, :, ln], a_scr, b_scr)
        h, leave = _lru_scan_slab(sl, a_scr, b_scr, carry_in[:, ln], True, tm)
        out_scr[sl] = hf_ref[0, :, ln] + h
        leaves.append(leave)
    leave = jnp.concatenate(leaves, -1)
    carry_scr[...] = jnp.broadcast_to(leave, carry_scr.shape)

    @pl.when(i == nt - 1)
    def _():
        hbf_ref[0] = leave

    for s in range(SUBLANES):
        ht = jnp.concatenate([out_scr[sl, pl.ds(s, seg, stride=SUBLANES), :]
                              for sl in range(nl)], -1)
        rows = slice(s * seg, (s + 1) * seg)
        o_ref[0, rows, :] = (ht * jax.nn.gelu(gate_ref[0, rows, :])).astype(BF16)


def _lru_bwd(xc, gate, hf, w_gates, bias, lam, h0):
    bsz, t, w = xc.shape
    tm = min(TOKEN_TILE, t)
    nt = t // tm
    tok = pl.BlockSpec((1, tm, w), lambda b, i: (b, nt - 1 - i, 0))
    state = pl.BlockSpec((1, 1, w), lambda b, i: (b, 0, 0))
    return pl.pallas_call(
        functools.partial(_lru_bwd_kernel, tm=tm, nt=nt),
        out_shape=[jax.ShapeDtypeStruct((bsz, t, w), BF16),
                   jax.ShapeDtypeStruct((bsz, 1, w), F32)],
        grid=(bsz, nt),
        in_specs=[tok, tok, tok, _const_spec(w_gates.shape), _const_spec(bias.shape),
                  _const_spec(lam.shape), pl.BlockSpec((1, 2, w), lambda b, i: (b, 0, 0))],
        out_specs=[tok, state],
        scratch_shapes=_lru_scratch(tm)[1:] + [pltpu.VMEM((w // LANES, tm, LANES), F32)],
        compiler_params=_params(2),
        name="lru_bwd",
    )(xc, gate, hf, w_gates, bias, lam, h0)


def _ret_kernel(logit_ref, q_ref, k_ref, v_ref, g_ref, ng_ref, nb_ref, s0f_ref, s0b_ref,
                o_ref, sff_ref, sbf_ref, dmat_scr, kv_scr, ss_scr, *, nchunks):
    hp = pl.program_id(1)
    c = RET_BLOCK
    dk = RET_DK
    contract_cols = (((1,), (1,)), ((), ()))

    def rows(n):
        return pl.ds(pl.multiple_of(n * c, c), c)

    def log_decay(direction, hd, shape):
        z = jnp.full(shape, logit_ref[direction, hd], F32)
        return -_softplus(-z)

    rowi = lax.broadcasted_iota(jnp.int32, (c, c), 0)
    coli = lax.broadcasted_iota(jnp.int32, (c, c), 1)
    diff = (rowi - coli).astype(F32)
    pos = lax.broadcasted_iota(jnp.int32, (c, dk), 0).astype(F32)
    lpos = lax.broadcasted_iota(jnp.int32, (dk, c), 1).astype(F32)

    heads = []
    for hh in range(RET_HEADS_PER_STEP):
        hd = hp * RET_HEADS_PER_STEP + hh
        lgf = log_decay(0, hd, (c, c))
        lgb = log_decay(1, hd, (c, c))
        dmat_scr[hh] = jnp.where(diff >= 0, jnp.exp(lgf * jnp.maximum(diff, 0.0)),
                                 jnp.exp(lgb * jnp.maximum(-diff, 0.0)))
        lgf_r, lgb_r = log_decay(0, hd, (c, dk)), log_decay(1, hd, (c, dk))
        heads.append(dict(
            lanes=slice(hh * dk, (hh + 1) * dk),
            qdec_f=jnp.exp(lgf_r * (pos + 1.0)), qdec_b=jnp.exp(lgb_r * (c - pos)),
            kdec_f=jnp.exp(log_decay(0, hd, (dk, c)) * (c - 1.0 - lpos)),
            kdec_b=jnp.exp(log_decay(1, hd, (dk, c)) * lpos),
            gc_f=jnp.exp(log_decay(0, hd, (dk, dk)) * float(c)),
            gc_b=jnp.exp(log_decay(1, hd, (dk, dk)) * float(c))))

    def pass_a(n, carry):
        for hh, hv in enumerate(heads):
            kt = k_ref[0, rows(n), hv["lanes"]].astype(F32).T
            lhs = jnp.concatenate([kt * hv["kdec_f"], kt * hv["kdec_b"]], 0).astype(BF16)
            kv_scr[hh, n] = jnp.dot(lhs, v_ref[0, rows(n), hv["lanes"]],
                                    preferred_element_type=F32)
        return carry

    lax.fori_loop(0, nchunks, pass_a, 0, unroll=min(nchunks, RET_UNROLL))

    for hh, hv in enumerate(heads):
        def pass_s(n, carry, hh=hh, hv=hv):
            sf, sb = carry
            m = nchunks - 1 - n
            ss_scr[hh, n, 0:dk] = sf.astype(BF16)
            ss_scr[hh, m, dk:2 * dk] = sb.astype(BF16)
            return (hv["gc_f"] * sf + kv_scr[hh, n, 0:dk],
                    hv["gc_b"] * sb + kv_scr[hh, m, dk:2 * dk])

        sf, sb = lax.fori_loop(0, nchunks, pass_s, (s0f_ref[0, hh], s0b_ref[0, hh]))
        sff_ref[0, hh] = sf
        sbf_ref[0, hh] = sb

    gn = ng_ref[...]
    bn = nb_ref[...]

    def pass_b(n, carry):
        for hh, hv in enumerate(heads):
            ln = hv["lanes"]
            qn = q_ref[0, rows(n), ln]
            sc = lax.dot_general(qn, k_ref[0, rows(n), ln], contract_cols,
                                 preferred_element_type=F32)
            o = jnp.dot((sc * dmat_scr[hh]).astype(BF16), v_ref[0, rows(n), ln],
                        preferred_element_type=F32)
            qf = qn.astype(F32)
            qd = jnp.concatenate([(qf * hv["qdec_f"]).astype(BF16),
                                  (qf * hv["qdec_b"]).astype(BF16)], -1)
            o = o + jnp.dot(qd, ss_scr[hh, n], preferred_element_type=F32)
            mu = jnp.mean(o, -1, keepdims=True)
            oc = o - mu
            var = jnp.mean(oc * oc, -1, keepdims=True)
            on = oc * lax.rsqrt(var + LN_EPS) * gn[:, ln] + bn[:, ln]
            hg = g_ref[0, rows(n), ln]
            o_ref[0, rows(n), ln] = (on * (hg * (jnp.tanh(hg) + 1.0))).astype(BF16)
        return carry

    lax.fori_loop(0, nchunks, pass_b, 0, unroll=min(nchunks, RET_UNROLL))


def _ret(logit, q, k, v, g, norm_g, norm_b, s0f, s0b):
    bsz, t, _ = q.shape
    nchunks = t // RET_BLOCK
    hps = RET_HEADS_PER_STEP
    wid = hps * RET_DK
    head = pl.BlockSpec((1, t, wid), lambda b, h: (b, 0, h))
    vec = pl.BlockSpec((1, wid), lambda b, h: (0, h))
    st = pl.BlockSpec((1, hps, RET_DK, RET_DK), lambda b, h: (b, h, 0, 0))
    st_shape = jax.ShapeDtypeStruct((bsz, RET_HEADS, RET_DK, RET_DK), F32)
    return pl.pallas_call(
        functools.partial(_ret_kernel, nchunks=nchunks),
        out_shape=[jax.ShapeDtypeStruct((bsz, t, RET_WIDTH), BF16), st_shape, st_shape],
        grid=(bsz, RET_HEADS // hps),
        in_specs=[pl.BlockSpec(memory_space=pltpu.SMEM), head, head, head, head, vec, vec, st, st],
        out_specs=[head, st, st],
        scratch_shapes=[pltpu.VMEM((hps, RET_BLOCK, RET_BLOCK), F32),
                        pltpu.VMEM((hps, nchunks, 2 * RET_DK, RET_DK), F32),
                        pltpu.VMEM((hps, nchunks, 2 * RET_DK, RET_DK), BF16)],
        compiler_params=_params(2),
        name="ret",
    )(logit, q, k, v, g, norm_g, norm_b, s0f, s0b)


def _rope_tables(t):
    n = RET_DK // 4
    inv = ROPE_BASE ** (-np.arange(n, dtype=np.float64) / n)
    pos = np.arange(t)
    a_r = (pos // GRID_W)[:, None] * inv
    a_c = (pos % GRID_W)[:, None] * inv
    ang = np.concatenate([a_r, a_r, a_c, a_c], -1)
    first = (np.arange(RET_DK) % (2 * n)) < n
    sin = np.sin(ang)
    tabs = (np.cos(ang), np.where(first, -sin, 0.0), np.where(first, 0.0, sin))
    return tuple(jnp.asarray(tab, F32) for tab in tabs)


def _block_diag(wb):
    nb, c, d = wb.shape
    eye = jnp.eye(nb, dtype=wb.dtype)
    return jnp.einsum('ncd,nm->ncmd', wb, eye).reshape(nb * c, nb * d)


def kernel(x, c, ctx, c_ctx, w_ada, b_ada, ffn1_w_gate, ffn1_w_up, ffn1_w_down, ffn2_w_gate,
           ffn2_w_up, ffn2_w_down, w_in, w_out, ret_decay_logit, ret_norm_g, ret_norm_b,
           lru_conv_w, lru_conv_b, lru_w_a, lru_b_a, lru_w_i, lru_b_i, lru_lambda, ln_g, ln_b):
    bsz, t, d = x.shape
    tc = ctx.shape[1]
    l = 0

    pad = jnp.zeros((2 * SUBLANES - bsz - 1, d), F32)
    m = _ada(jnp.concatenate([c, c_ctx[None], pad], 0), w_ada[l], b_ada[l])
    mods_lat = m[:bsz].reshape(bsz, 3 * N_SUB, d)
    mods_ctx = jnp.broadcast_to(m[bsz].reshape(1, 3 * N_SUB, d), (bsz, 3 * N_SUB, d))

    ffn1_w = (ffn1_w_gate[l].astype(BF16), ffn1_w_up[l].astype(BF16), ffn1_w_down[l].astype(BF16))
    ffn2_w = (ffn2_w_gate[l].astype(BF16), ffn2_w_up[l].astype(BF16), ffn2_w_down[l].astype(BF16))
    col_scale = jnp.ones((w_in.shape[-1],), F32).at[G_SEG * SEG:(G_SEG + 1) * SEG].set(0.5)
    w_in_b = (w_in[l] * col_scale).astype(BF16)
    w_out_b = w_out[l].astype(BF16)
    lng, lnb = ln_g[l], ln_b[l]

    nl = LRU_WIDTH // LANES
    per_slab = LRU_BLOCKS // nl

    def slab_blocks(wb):
        wb = wb.reshape(2 * nl, per_slab, wb.shape[-2], wb.shape[-1])
        return jax.vmap(_block_diag)(wb).reshape(2, nl, LANES, LANES)

    w_gates = (0.5 * jnp.concatenate([slab_blocks(lru_w_a[l]), slab_blocks(lru_w_i[l])], -1)
               ).astype(BF16)
    gate_bias = 0.5 * jnp.concatenate([lru_b_a[l].reshape(2, nl, 1, LANES),
                                       lru_b_i[l].reshape(2, nl, 1, LANES)], -1)
    lam = lru_lambda[l][:, None, :]
    lru_w = dict(w_gates=w_gates, bias=gate_bias, lam=lam)
    conv_w = dict(conv_w=lru_conv_w[l], conv_b=lru_conv_b[l][None, :])
    norm_g = ret_norm_g[l][None, :]
    norm_b = ret_norm_b[l][None, :]
    logit = ret_decay_logit[l]

    tables_lat = _rope_tables(t)
    tables_ctx = (jnp.ones((tc, RET_DK), F32), jnp.zeros((tc, RET_DK), F32),
                  jnp.zeros((tc, RET_DK), F32))

    hc = _ffn(ctx, mods_ctx, *ffn1_w, lng, lnb, mod_idx=(0, 1, 2), ln_idx=0)
    h0_zero = jnp.zeros((bsz, 2, LRU_WIDTH), F32)
    kc, vc, qc, gc, gatec, xcc, hfc, hcf_last = _inproj(hc, mods_ctx, w_in_b, tables_ctx,
                                                        h0=h0_zero, **conv_w, **lru_w)
    _, hcb_first = _lru_bwd(xcc, gatec, hfc, h0=h0_zero, **lru_w)
    s_zero = jnp.zeros((bsz, RET_HEADS, RET_DK, RET_DK), F32)
    _, s_cf, s_cb = _ret(logit, qc, kc, vc, gc, norm_g, norm_b, s_zero, s_zero)

    h0 = jnp.concatenate([hcf_last, hcb_first], 1)
    x1 = _ffn(x, mods_lat, *ffn1_w, lng, lnb, mod_idx=(0, 1, 2), ln_idx=0)
    k, v, q, g, gate, xc, hf, _ = _inproj(x1, mods_lat, w_in_b, tables_lat, h0=h0,
                                          **conv_w, **lru_w)
    lru_out, _ = _lru_bwd(xc, gate, hf, h0=h0, **lru_w)
    ret_out, _, _ = _ret(logit, q, k, v, g, norm_g, norm_b, s_cf, s_cb)
    return _ffn(x1, mods_lat, *ffn2_w, lng, lnb, mod_idx=(6, 7, 8), ln_idx=2,
                mix=(5, 1), mix_in=(ret_out, lru_out, w_out_b))
```

```python
import functools

import jax
import jax.numpy as jnp
import numpy as np
from jax import lax
from jax.experimental import pallas as pl
from jax.experimental.pallas import tpu as pltpu

F32 = jnp.float32
BF16 = jnp.bfloat16

D_MODEL = 1024
D_FF = 2816
N_SUB = 3
RET_WIDTH = 512
RET_HEADS = 4
RET_DK = RET_WIDTH // RET_HEADS
RET_BLOCK = 256
RET_HEADS_PER_STEP = 2
RET_UNROLL = 4
LRU_WIDTH = 512
LRU_BLOCKS = 8
LRU_C = 8.0
CONV_W = 4
CONV_PAD_LO = 2
GRID_W = 64
DEPTH = 1
MACARON = 0.5
ALPHA = (2.0 * DEPTH) ** 0.25
ROPE_BASE = 10000.0
LN_EPS = 1e-5
K_SCALE = RET_DK ** -0.5
GELU_K0 = 2.0 * (2.0 / np.pi) ** 0.5
GELU_K1 = 8.0 * 0.044715 * (2.0 / np.pi) ** 0.5

SEG = 512
K_SEG, V_SEG, X_SEG, Q_SEG, G_SEG, GATE_SEG = range(6)

LANES = 128
SUBLANES = 8
BF16_ROWS = 16
MXU_DIM = 256
VMEM_LIMIT_BYTES = 56 * 1024 * 1024

FF_CHUNK = MXU_DIM
N_FF_CHUNKS = D_FF // FF_CHUNK
TOKEN_TILE = 512
FFN_SUB_TILE = 512
ADA_COL_TILE = 1024
WEIGHT_CAST_STEPS = 8
HALO = SUBLANES
LRU_PRE = CONV_PAD_LO * SUBLANES
LRU_EXT = (CONV_W - 1) * SUBLANES


def _layer_norm(y, g, b):
    mu = jnp.mean(y, -1, keepdims=True)
    yc = y - mu
    var = jnp.mean(yc * yc, -1, keepdims=True)
    return yc * lax.rsqrt(var + LN_EPS) * g + b


def _sigmoid(x):
    return 0.5 * jnp.tanh(0.5 * x) + 0.5


def _softplus(z):
    return jnp.maximum(z, 0.0) + jnp.log(1.0 + jnp.exp(-jnp.abs(z)))


def _const_spec(shape):
    zeros = (0,) * len(shape)
    return pl.BlockSpec(shape, lambda *_: zeros, pipeline_mode=pl.Buffered(1))


def _params(n_grid_axes):
    return pltpu.CompilerParams(dimension_semantics=("arbitrary",) * n_grid_axes,
                                vmem_limit_bytes=VMEM_LIMIT_BYTES)


def _cast_kernel(*refs, scaled):
    n = len(scaled)
    ins, outs = refs[:len(refs) - n], refs[len(refs) - n:]
    src = iter(ins)
    for o_ref, has_scale in zip(outs, scaled):
        w = next(src)[...]
        if has_scale:
            w = w * next(src)[...]
        o_ref[...] = w.astype(BF16)


def _to_bf16(weights, col_scales):
    steps = WEIGHT_CAST_STEPS
    args, in_specs, out_specs, out_shape = [], [], [], []
    for w, cs in zip(weights, col_scales):
        rows, cols = w.shape
        blk = pl.BlockSpec((rows // steps, cols), lambda i: (i, 0))
        args.append(w)
        in_specs.append(blk)
        if cs is not None:
            args.append(cs)
            in_specs.append(pl.BlockSpec((1, cols), lambda i: (0, 0)))
        out_specs.append(blk)
        out_shape.append(jax.ShapeDtypeStruct(w.shape, BF16))
    return pl.pallas_call(
        functools.partial(_cast_kernel, scaled=tuple(cs is not None for cs in col_scales)),
        out_shape=out_shape, grid=(steps,), in_specs=in_specs, out_specs=out_specs,
        compiler_params=_params(1), name="cast",
    )(*args)


def _ada_kernel(c_ref, w_ref, b_ref, o_ref):
    c = c_ref[...]
    s = c * _sigmoid(c)
    o_ref[...] = jnp.dot(s, w_ref[...], preferred_element_type=F32) + b_ref[...]


def _ada(cc, w, b):
    rows, d = cc.shape
    n = w.shape[1]
    tn = ADA_COL_TILE
    return pl.pallas_call(
        _ada_kernel,
        out_shape=jax.ShapeDtypeStruct((rows, n), F32),
        grid=(n // tn,),
        in_specs=[pl.BlockSpec((rows, d), lambda j: (0, 0)),
                  pl.BlockSpec((d, tn), lambda j: (0, j)),
                  pl.BlockSpec((1, tn), lambda j: (0, j))],
        out_specs=pl.BlockSpec((rows, tn), lambda j: (0, j)),
        compiler_params=_params(1),
        name="ada",
    )(cc, w, b.reshape(1, n))


def _ffn_kernel(*refs, mod_idx, ln_idx, mix, tm):
    if mix is None:
        x_ref, mod_ref, wg_ref, wu_ref, wd_ref, lng_ref, lnb_ref, o_ref, a_scr = refs
    else:
        (x_ref, ret_ref, lru_ref, wo_ref, mod_ref, wg_ref, wu_ref, wd_ref, lng_ref, lnb_ref,
         o_ref, a_scr) = refs
    m = mod_ref[0]
    i_shift, i_scale, i_gate = mod_idx
    sub = min(tm, FFN_SUB_TILE)
    for st in range(tm // sub):
        rows = slice(st * sub, (st + 1) * sub)
        h = x_ref[0, rows, :]
        if mix is not None:
            gate_idx, mix_ln_idx = mix
            z = jnp.concatenate([ret_ref[0, rows, :], lru_ref[0, rows, :]], -1)
            y = jnp.dot(z, wo_ref[...], preferred_element_type=F32)
            h = _layer_norm(ALPHA * h + m[gate_idx:gate_idx + 1] * y,
                            lng_ref[mix_ln_idx:mix_ln_idx + 1], lnb_ref[mix_ln_idx:mix_ln_idx + 1])
        u = (h * (1.0 + m[i_scale:i_scale + 1]) + m[i_shift:i_shift + 1]).astype(BF16)
        for j in range(N_FF_CHUNKS):
            cols = slice(j * FF_CHUNK, (j + 1) * FF_CHUNK)
            g = jnp.dot(u, wg_ref[:, cols], preferred_element_type=F32)
            p = jnp.dot(u, wu_ref[:, cols], preferred_element_type=F32)
            a_scr[rows, cols] = (g * _sigmoid(g) * p).astype(BF16)
        f = jnp.dot(a_scr[rows, :], wd_ref[...], preferred_element_type=F32)
        y = ALPHA * h + (MACARON * m[i_gate:i_gate + 1]) * f
        o_ref[0, rows, :] = _layer_norm(y, lng_ref[ln_idx:ln_idx + 1], lnb_ref[ln_idx:ln_idx + 1])


def _ffn(x, mods, wg, wu, wd, ln_g, ln_b, *, mod_idx, ln_idx, mix=None, mix_in=None):
    bsz, t, d = x.shape
    tm = min(TOKEN_TILE, t)
    tok = lambda w: pl.BlockSpec((1, tm, w), lambda b, i: (b, i, 0))
    in_specs = [tok(d)]
    args = [x]
    if mix is not None:
        ret_out, lru_out, w_out = mix_in
        in_specs += [tok(RET_WIDTH), tok(LRU_WIDTH), _const_spec(w_out.shape)]
        args += [ret_out, lru_out, w_out]
    in_specs += [pl.BlockSpec((1, 3 * N_SUB, d), lambda b, i: (b, 0, 0)),
                 _const_spec(wg.shape), _const_spec(wu.shape), _const_spec(wd.shape),
                 _const_spec(ln_g.shape), _const_spec(ln_b.shape)]
    args += [mods, wg, wu, wd, ln_g, ln_b]
    return pl.pallas_call(
        functools.partial(_ffn_kernel, mod_idx=mod_idx, ln_idx=ln_idx, mix=mix, tm=tm),
        out_shape=jax.ShapeDtypeStruct((bsz, t, d), F32),
        grid=(bsz, t // tm),
        in_specs=in_specs,
        out_specs=tok(d),
        scratch_shapes=[pltpu.VMEM((tm, D_FF), BF16)],
        compiler_params=_params(2),
        name="ffn_mix" if mix is not None else "ffn",
    )(*args)


def _lru_conv_slab(sl, xl, before, after, cw_ref, cb_ref, xe_scr, tm):
    seg = tm // SUBLANES
    pre = LRU_PRE
    ln = slice(sl * LANES, (sl + 1) * LANES)
    row1 = lax.broadcasted_iota(jnp.int32, (SUBLANES, LANES), 0)
    for s in range(SUBLANES):
        xe_scr[sl, pl.ds(pre + s, seg, stride=SUBLANES), :] = xl[s * seg:(s + 1) * seg, ln]
    for d in range(CONV_PAD_LO):
        src = pre + (seg - 1 - d) * SUBLANES
        edge = jnp.broadcast_to(before[HALO - 1 - d:HALO - d, ln], (SUBLANES, LANES))
        xe_scr[sl, pre - (d + 1) * SUBLANES:pre - d * SUBLANES, :] = jnp.where(
            row1 == 0, edge, pltpu.roll(xe_scr[sl, src:src + SUBLANES, :], 1, 0))
    edge = jnp.broadcast_to(after[0:1, ln], (SUBLANES, LANES))
    xe_scr[sl, pre + tm:pre + tm + SUBLANES, :] = jnp.where(
        row1 == SUBLANES - 1, edge, pltpu.roll(xe_scr[sl, pre:pre + SUBLANES, :], SUBLANES - 1, 0))
    acc = cb_ref[:, ln]
    for k in range(CONV_W):
        off = pre + (k - CONV_PAD_LO) * SUBLANES
        acc = acc + xe_scr[sl, off:off + tm, :] * cw_ref[k:k + 1, ln]
    return acc


def _lru_gates_slab(xc, w_gates):
    return jnp.dot(xc.astype(BF16), w_gates, preferred_element_type=F32)


def _lru_coeffs_slab(sl, xc, gi, bias, lam, a_scr, b_scr):
    th = jnp.tanh(gi + bias)
    th_r = th[:, :LANES]
    th_i = th[:, LANES:]
    c = (-0.5 * LRU_C) * _softplus(-lam)
    log_a = c * th_r + c
    a = jnp.exp(log_a)
    a_scr[sl] = a
    t4 = (-0.25 * jnp.tanh(log_a)) * (1.0 + a * a)
    half_root = jnp.where(t4 > 0.0, t4 * lax.rsqrt(t4), 0.0)
    b_scr[sl] = (half_root * xc) * (th_i + 1.0)


def _sublane_scan(a, b, hprev, reverse):
    row = lax.broadcasted_iota(jnp.int32, a.shape, 0)
    for k in (1, 2, 4):
        if reverse:
            sh = SUBLANES - k
            msk = row < sh
        else:
            sh = k
            msk = row >= k
        a_s = pltpu.roll(a, sh, 0)
        b_s = pltpu.roll(b, sh, 0)
        b = jnp.where(msk, a * b_s + b, b)
        a = jnp.where(msk, a * a_s, a)
    return a * hprev + b


def _lru_scan_slab(sl, a_scr, b_scr, carry_in, reverse, tm):
    seg = tm // SUBLANES
    row = lax.broadcasted_iota(jnp.int32, (SUBLANES, LANES), 0)
    hh = jnp.zeros((SUBLANES, LANES), F32)
    pp = jnp.ones((SUBLANES, LANES), F32)
    for jj in range(seg):
        j = seg - 1 - jj if reverse else jj
        rows = slice(j * SUBLANES, (j + 1) * SUBLANES)
        aj = a_scr[sl, rows, :]
        hh = aj * hh + b_scr[sl, rows, :]
        pp = aj * pp
        b_scr[sl, rows, :] = hh
        a_scr[sl, rows, :] = pp
    ends = _sublane_scan(pp, hh, carry_in, reverse)
    if reverse:
        enter = jnp.where(row == SUBLANES - 1, carry_in, pltpu.roll(ends, SUBLANES - 1, 0))
        leave = ends[0:1]
    else:
        enter = jnp.where(row == 0, carry_in, pltpu.roll(ends, 1, 0))
        leave = ends[SUBLANES - 1:SUBLANES]
    h = (b_scr[sl].reshape(seg, SUBLANES, LANES)
         + a_scr[sl].reshape(seg, SUBLANES, LANES) * enter[None]).reshape(tm, LANES)
    return h, leave


def _lru_scratch(tm):
    nl = LRU_WIDTH // LANES
    return [pltpu.VMEM((nl, tm + LRU_EXT, LANES), F32),
            pltpu.VMEM((nl, tm, LANES), F32),
            pltpu.VMEM((nl, tm, LANES), F32),
            pltpu.VMEM((SUBLANES, LRU_WIDTH), F32)]


def _inproj_kernel(x_ref, mod_ref, w_ref, cos_ref, sa_ref, sb_ref,
                   cw_ref, cb_ref, wg_ref, bias_ref, lam_ref, h0_ref, *refs, tm, nt, states_only):
    if states_only:
        k_ref, v_ref, xc_ref, hf_ref, hfl_ref = refs[:5]
    else:
        k_ref, v_ref, q_ref, g_ref, gate_ref, xc_ref, hf_ref, hfl_ref = refs[:8]
    xe_scr, a_scr, b_scr, carry_scr, xl_scr, tail_scr = refs[-6:]
    s = pl.program_id(0)
    ip = (jnp.maximum(s, 1) - 1) % nt

    @pl.when(s == 0)
    def _():
        xl_scr[...] = jnp.zeros_like(xl_scr)
        tail_scr[...] = jnp.zeros_like(tail_scr)
        carry_scr[...] = jnp.zeros_like(carry_scr)

    m = mod_ref[0]
    u = (x_ref[0] * (1.0 + m[4:5]) + m[3:4]).astype(BF16)
    cos = cos_ref[...]
    sa = sa_ref[...]
    sb = sb_ref[...]

    def seg(j):
        return jnp.dot(u, w_ref[:, j * SEG:(j + 1) * SEG], preferred_element_type=F32)

    def rope(t):
        outs = []
        for hd in range(RET_HEADS):
            th = t[:, hd * RET_DK:(hd + 1) * RET_DK]
            outs.append(th * cos + pltpu.roll(th, RET_DK - 32, 1) * sa + pltpu.roll(th, 32, 1) * sb)
        return jnp.concatenate(outs, -1)

    xl_new = seg(X_SEG)

    nl = LRU_WIDTH // LANES
    xl_old = xl_scr[...]
    before = jnp.where(ip > 0, tail_scr[...], 0.0)
    after = jnp.where(ip < nt - 1, xl_new[0:HALO], 0.0)
    carry_in = jnp.where(ip == 0, jnp.broadcast_to(h0_ref[0, 0:1], carry_scr.shape), carry_scr[...])

    xc = [_lru_conv_slab(sl, xl_old, before, after, cw_ref, cb_ref, xe_scr, tm) for sl in range(nl)]
    for sl in range(nl):
        xc_ref[0, :, sl * LANES:(sl + 1) * LANES] = xc[sl]
    tail_scr[...] = xl_old[tm - HALO:tm]
    xl_scr[...] = xl_new
    gi = [_lru_gates_slab(xc[sl], wg_ref[0, sl]) for sl in range(nl)]

    k_ref[0] = (rope(seg(K_SEG)) * K_SCALE).astype(BF16)
    v_ref[0] = seg(V_SEG).astype(BF16)
    if not states_only:
        q_ref[0] = rope(seg(Q_SEG)).astype(BF16)
        g_ref[0] = seg(G_SEG)
        gate_ref[0] = seg(GATE_SEG)

    leaves = []
    for sl in range(nl):
        ln = slice(sl * LANES, (sl + 1) * LANES)
        _lru_coeffs_slab(sl, xc[sl], gi[sl], bias_ref[0, sl], lam_ref[0, :, ln], a_scr, b_scr)
        h, leave = _lru_scan_slab(sl, a_scr, b_scr, carry_in[:, ln], False, tm)
        hf_ref[0, :, ln] = h
        leaves.append(leave)
    leave = jnp.concatenate(leaves, -1)
    carry_scr[...] = jnp.broadcast_to(leave, carry_scr.shape)

    @pl.when(ip == nt - 1)
    def _():
        hfl_ref[0] = leave


def _inproj(x, mods, w_in, tables, conv_w, conv_b, w_gates, bias, lam, h0, states_only=False):
    bsz, t, d = x.shape
    tm = min(TOKEN_TILE, t)
    nt = t // tm
    ntiles = bsz * nt
    cur = lambda s: jnp.minimum(s, ntiles - 1)
    old = lambda s: jnp.maximum(s, 1) - 1
    tok = lambda w: pl.BlockSpec((1, tm, w), lambda s: (cur(s) // nt, cur(s) % nt, 0))
    tab = pl.BlockSpec((tm, RET_DK), lambda s: (cur(s) % nt, 0))
    shp = lambda dt: jax.ShapeDtypeStruct((bsz, t, SEG), dt)
    cur_dtypes = [BF16, BF16] if states_only else [BF16, BF16, BF16, F32, F32]
    return pl.pallas_call(
        functools.partial(_inproj_kernel, tm=tm, nt=nt, states_only=states_only),
        out_shape=[shp(dt) for dt in cur_dtypes] + [shp(F32), shp(F32),
                                                    jax.ShapeDtypeStruct((bsz, 1, LRU_WIDTH), F32)],
        grid=(ntiles + 1,),
        in_specs=[tok(d), pl.BlockSpec((1, 3 * N_SUB, d), lambda s: (cur(s) // nt, 0, 0)),
                  _const_spec(w_in.shape), tab, tab, tab,
                  _const_spec(conv_w.shape), _const_spec(conv_b.shape), _const_spec(w_gates.shape),
                  _const_spec(bias.shape), _const_spec(lam.shape),
                  pl.BlockSpec((1, 2, LRU_WIDTH), lambda s: (old(s) // nt, 0, 0))],
        out_specs=[tok(SEG)] * len(cur_dtypes)
        + [pl.BlockSpec((1, tm, LRU_WIDTH), lambda s: (old(s) // nt, old(s) % nt, 0))] * 2
        + [pl.BlockSpec((1, 1, LRU_WIDTH), lambda s: (old(s) // nt, 0, 0))],
        scratch_shapes=_lru_scratch(tm) + [pltpu.VMEM((tm, LRU_WIDTH), F32),
                                           pltpu.VMEM((HALO, LRU_WIDTH), F32)],
        compiler_params=_params(1),
        name="inproj",
    )(x, mods, w_in, *tables, conv_w, conv_b, w_gates, bias, lam, h0)


def _lru_bwd_kernel(*refs, tm, nt, state_only):
    if state_only:
        xc_ref, wg_ref, bias_ref, lam_ref, h0_ref, hbf_ref, a_scr, b_scr, carry_scr = refs
    else:
        (xc_ref, gate_ref, hf_ref, wg_ref, bias_ref, lam_ref, h0_ref,
         o_ref, hbf_ref, a_scr, b_scr, carry_scr, out_scr) = refs
    i = pl.program_id(1)
    nl = LRU_WIDTH // LANES
    seg = tm // SUBLANES

    @pl.when(i == 0)
    def _():
        carry_scr[...] = jnp.broadcast_to(h0_ref[0, 1:2], carry_scr.shape)

    carry_in = carry_scr[...]
    leaves = []
    for sl in range(nl):
        ln = slice(sl * LANES, (sl + 1) * LANES)
        xc = xc_ref[0, :, ln]
        gi = _lru_gates_slab(xc, wg_ref[1, sl])
        _lru_coeffs_slab(sl, xc, gi, bias_ref[1, sl], lam_ref[1, :, ln], a_scr, b_scr)
        h, leave = _lru_scan_slab(sl, a_scr, b_scr, carry_in[:, ln], True, tm)
        if not state_only:
            out_scr[sl] = hf_ref[0, :, ln] + h
        leaves.append(leave)
    leave = jnp.concatenate(leaves, -1)
    carry_scr[...] = jnp.broadcast_to(leave, carry_scr.shape)

    @pl.when(i == nt - 1)
    def _():
        hbf_ref[0] = leave

    if state_only:
        return
    for s in range(SUBLANES):
        ht = jnp.concatenate([out_scr[sl, pl.ds(s, seg, stride=SUBLANES), :]
                              for sl in range(nl)], -1)
        rows = slice(s * seg, (s + 1) * seg)
        hx = gate_ref[0, rows, :]
        gelu = hx * (1.0 + jnp.tanh(hx * (GELU_K0 + GELU_K1 * (hx * hx))))
        o_ref[0, rows, :] = (ht * gelu).astype(BF16)


def _lru_bwd(xc, gate_hf, w_gates, bias, lam, h0):
    bsz, t, w = xc.shape
    tm = min(TOKEN_TILE, t)
    nt = t // tm
    state_only = gate_hf is None
    tok = pl.BlockSpec((1, tm, w), lambda b, i: (b, nt - 1 - i, 0))
    state = pl.BlockSpec((1, 1, w), lambda b, i: (b, 0, 0))
    st_shape = jax.ShapeDtypeStruct((bsz, 1, w), F32)
    consts = [_const_spec(w_gates.shape), _const_spec(bias.shape), _const_spec(lam.shape),
              pl.BlockSpec((1, 2, w), lambda b, i: (b, 0, 0))]
    scratch = _lru_scratch(tm)[1:]
    if state_only:
        args, in_specs, out_specs, out_shape = [xc], [tok], state, st_shape
    else:
        args, in_specs = [xc, *gate_hf], [tok, tok, tok]
        out_specs, out_shape = [tok, state], [jax.ShapeDtypeStruct((bsz, t, w), BF16), st_shape]
        scratch = scratch + [pltpu.VMEM((w // LANES, tm, LANES), F32)]
    return pl.pallas_call(
        functools.partial(_lru_bwd_kernel, tm=tm, nt=nt, state_only=state_only),
        out_shape=out_shape, grid=(bsz, nt), in_specs=in_specs + consts, out_specs=out_specs,
        scratch_shapes=scratch, compiler_params=_params(2), name="lru_bwd",
    )(*args, w_gates, bias, lam, h0)


def _ret_kernel(*refs, nchunks, states_only):
    if states_only:
        logit_ref, k_ref, v_ref, s0f_ref, s0b_ref, sff_ref, sbf_ref, kv_scr = refs
    else:
        (logit_ref, k_ref, v_ref, s0f_ref, s0b_ref, q_ref, g_ref, ng_ref, nb_ref,
         o_ref, kv_scr, dmat_scr, ss_scr) = refs
    hp = pl.program_id(1)
    c = RET_BLOCK
    dk = RET_DK
    contract_cols = (((1,), (1,)), ((), ()))

    def rows(n):
        return pl.ds(pl.multiple_of(n * c, c), c)

    def log_decay(direction, hd, shape):
        z = jnp.full(shape, logit_ref[direction, hd], F32)
        return -_softplus(-z)

    rowi = lax.broadcasted_iota(jnp.int32, (c, c), 0)
    coli = lax.broadcasted_iota(jnp.int32, (c, c), 1)
    diff = (rowi - coli).astype(F32)
    pos = lax.broadcasted_iota(jnp.int32, (c, dk), 0).astype(F32)
    lpos = lax.broadcasted_iota(jnp.int32, (dk, c), 1).astype(F32)

    heads = []
    for hh in range(RET_HEADS_PER_STEP):
        hd = hp * RET_HEADS_PER_STEP + hh
        if not states_only:
            lgf = log_decay(0, hd, (c, c))
            lgb = log_decay(1, hd, (c, c))
            dmat_scr[hh] = jnp.where(diff >= 0, jnp.exp(lgf * jnp.maximum(diff, 0.0)),
                                     jnp.exp(lgb * jnp.maximum(-diff, 0.0)))
        lgf_r, lgb_r = log_decay(0, hd, (c, dk)), log_decay(1, hd, (c, dk))
        heads.append(dict(
            lanes=slice(hh * dk, (hh + 1) * dk),
            qdec_f=jnp.exp(lgf_r * (pos + 1.0)), qdec_b=jnp.exp(lgb_r * (c - pos)),
            kdec_f=jnp.exp(log_decay(0, hd, (dk, c)) * (c - 1.0 - lpos)),
            kdec_b=jnp.exp(log_decay(1, hd, (dk, c)) * lpos),
            gc_f=jnp.exp(log_decay(0, hd, (dk, dk)) * float(c)),
            gc_b=jnp.exp(log_decay(1, hd, (dk, dk)) * float(c))))

    def pass_a(n, carry):
        for hh, hv in enumerate(heads):
            kt = k_ref[0, rows(n), hv["lanes"]].astype(F32).T
            lhs = jnp.concatenate([kt * hv["kdec_f"], kt * hv["kdec_b"]], 0).astype(BF16)
            kv_scr[hh, n] = jnp.dot(lhs, v_ref[0, rows(n), hv["lanes"]],
                                    preferred_element_type=F32)
        return carry

    lax.fori_loop(0, nchunks, pass_a, 0, unroll=min(nchunks, RET_UNROLL))

    for hh, hv in enumerate(heads):
        def pass_s(n, carry, hh=hh, hv=hv):
            sf, sb = carry
            m = nchunks - 1 - n
            if not states_only:
                ss_scr[hh, n, 0:dk] = sf.astype(BF16)
                ss_scr[hh, m, dk:2 * dk] = sb.astype(BF16)
            return (hv["gc_f"] * sf + kv_scr[hh, n, 0:dk],
                    hv["gc_b"] * sb + kv_scr[hh, m, dk:2 * dk])

        sf, sb = lax.fori_loop(0, nchunks, pass_s, (s0f_ref[0, hh], s0b_ref[0, hh]))
        if states_only:
            sff_ref[0, hh] = sf
            sbf_ref[0, hh] = sb

    if states_only:
        return
    gn = ng_ref[...]
    bn = nb_ref[...]

    def pass_b(n, carry):
        for hh, hv in enumerate(heads):
            ln = hv["lanes"]
            qn = q_ref[0, rows(n), ln]
            sc = lax.dot_general(qn, k_ref[0, rows(n), ln], contract_cols,
                                 preferred_element_type=F32)
            o = jnp.dot((sc * dmat_scr[hh]).astype(BF16), v_ref[0, rows(n), ln],
                        preferred_element_type=F32)
            qf = qn.astype(F32)
            qd = jnp.concatenate([(qf * hv["qdec_f"]).astype(BF16),
                                  (qf * hv["qdec_b"]).astype(BF16)], -1)
            o = o + jnp.dot(qd, ss_scr[hh, n], preferred_element_type=F32)
            mu = jnp.mean(o, -1, keepdims=True)
            oc = o - mu
            var = jnp.mean(oc * oc, -1, keepdims=True)
            on = oc * lax.rsqrt(var + LN_EPS) * gn[:, ln] + bn[:, ln]
            hg = g_ref[0, rows(n), ln]
            o_ref[0, rows(n), ln] = (on * (hg * (jnp.tanh(hg) + 1.0))).astype(BF16)
        return carry

    lax.fori_loop(0, nchunks, pass_b, 0, unroll=min(nchunks, RET_UNROLL))


def _ret(logit, k, v, s0f, s0b, mix=None):
    bsz, t, _ = k.shape
    nchunks = t // RET_BLOCK
    hps = RET_HEADS_PER_STEP
    wid = hps * RET_DK
    states_only = mix is None
    head = pl.BlockSpec((1, t, wid), lambda b, h: (b, 0, h))
    vec = pl.BlockSpec((1, wid), lambda b, h: (0, h))
    st = pl.BlockSpec((1, hps, RET_DK, RET_DK), lambda b, h: (b, h, 0, 0))
    st_shape = jax.ShapeDtypeStruct((bsz, RET_HEADS, RET_DK, RET_DK), F32)
    args = [logit, k, v, s0f, s0b]
    in_specs = [pl.BlockSpec(memory_space=pltpu.SMEM), head, head, st, st]
    scratch = [pltpu.VMEM((hps, nchunks, 2 * RET_DK, RET_DK), F32)]
    if states_only:
        out_specs, out_shape = [st, st], [st_shape, st_shape]
    else:
        args += list(mix)
        in_specs += [head, head, vec, vec]
        out_specs, out_shape = head, jax.ShapeDtypeStruct((bsz, t, RET_WIDTH), BF16)
        scratch += [pltpu.VMEM((hps, RET_BLOCK, RET_BLOCK), F32),
                    pltpu.VMEM((hps, nchunks, 2 * RET_DK, RET_DK), BF16)]
    return pl.pallas_call(
        functools.partial(_ret_kernel, nchunks=nchunks, states_only=states_only),
        out_shape=out_shape, grid=(bsz, RET_HEADS // hps), in_specs=in_specs, out_specs=out_specs,
        scratch_shapes=scratch, compiler_params=_params(2), name="ret",
    )(*args)


def _rope_tables(t):
    n = RET_DK // 4
    inv = ROPE_BASE ** (-np.arange(n, dtype=np.float64) / n)
    pos = np.arange(t)
    a_r = (pos // GRID_W)[:, None] * inv
    a_c = (pos % GRID_W)[:, None] * inv
    ang = np.concatenate([a_r, a_r, a_c, a_c], -1)
    first = (np.arange(RET_DK) % (2 * n)) < n
    sin = np.sin(ang)
    tabs = (np.cos(ang), np.where(first, -sin, 0.0), np.where(first, 0.0, sin))
    return tuple(jnp.asarray(tab, F32) for tab in tabs)


def _block_diag(wb):
    nb, c, d = wb.shape
    eye = jnp.eye(nb, dtype=wb.dtype)
    return jnp.einsum('ncd,nm->ncmd', wb, eye).reshape(nb * c, nb * d)


def kernel(x, c, ctx, c_ctx, w_ada, b_ada, ffn1_w_gate, ffn1_w_up, ffn1_w_down, ffn2_w_gate,
           ffn2_w_up, ffn2_w_down, w_in, w_out, ret_decay_logit, ret_norm_g, ret_norm_b,
           lru_conv_w, lru_conv_b, lru_w_a, lru_b_a, lru_w_i, lru_b_i, lru_lambda, ln_g, ln_b):
    bsz, t, d = x.shape
    tc = ctx.shape[1]
    l = 0

    pad = jnp.zeros((2 * SUBLANES - bsz - 1, d), F32)
    m = _ada(jnp.concatenate([c, c_ctx[None], pad], 0), w_ada[l], b_ada[l])
    mods_lat = m[:bsz].reshape(bsz, 3 * N_SUB, d)
    mods_ctx = jnp.broadcast_to(m[bsz].reshape(1, 3 * N_SUB, d), (bsz, 3 * N_SUB, d))

    col_scale = jnp.ones((1, w_in.shape[-1]), F32).at[:, G_SEG * SEG:(GATE_SEG + 1) * SEG].set(0.5)
    weights = [ffn1_w_gate[l], ffn1_w_up[l], ffn1_w_down[l],
               ffn2_w_gate[l], ffn2_w_up[l], ffn2_w_down[l], w_in[l], w_out[l]]
    weights = _to_bf16(weights, [None] * 6 + [col_scale, None])
    ffn1_w, ffn2_w, w_in_b, w_out_b = weights[0:3], weights[3:6], weights[6], weights[7]
    lng, lnb = ln_g[l], ln_b[l]

    nl = LRU_WIDTH // LANES
    per_slab = LRU_BLOCKS // nl

    def slab_blocks(wb):
        wb = wb.reshape(2 * nl, per_slab, wb.shape[-2], wb.shape[-1])
        return jax.vmap(_block_diag)(wb).reshape(2, nl, LANES, LANES)

    w_gates = (0.5 * jnp.concatenate([slab_blocks(lru_w_a[l]), slab_blocks(lru_w_i[l])], -1)
               ).astype(BF16)
    gate_bias = 0.5 * jnp.concatenate([lru_b_a[l].reshape(2, nl, 1, LANES),
                                       lru_b_i[l].reshape(2, nl, 1, LANES)], -1)
    lam = lru_lambda[l][:, None, :]
    lru_w = dict(w_gates=w_gates, bias=gate_bias, lam=lam)
    conv_w = dict(conv_w=lru_conv_w[l], conv_b=lru_conv_b[l][None, :])
    norm_g = ret_norm_g[l][None, :]
    norm_b = ret_norm_b[l][None, :]
    logit = ret_decay_logit[l]

    tables_lat = _rope_tables(t)
    tables_ctx = (jnp.ones((tc, RET_DK), F32), jnp.zeros((tc, RET_DK), F32),
                  jnp.zeros((tc, RET_DK), F32))

    rows_per_tile = max(TOKEN_TILE // tc, 1)
    hc = _ffn(ctx.reshape(bsz // rows_per_tile, rows_per_tile * tc, d), mods_ctx[:bsz // rows_per_tile],
              *ffn1_w, lng, lnb, mod_idx=(0, 1, 2), ln_idx=0).reshape(bsz, tc, d)
    h0_zero = jnp.zeros((bsz, 2, LRU_WIDTH), F32)
    kc, vc, xcc, _, hcf_last = _inproj(hc, mods_ctx, w_in_b, tables_ctx, h0=h0_zero,
                                       states_only=True, **conv_w, **lru_w)
    hcb_first = _lru_bwd(xcc, None, h0=h0_zero, **lru_w)
    s_zero = jnp.zeros((bsz, RET_HEADS, RET_DK, RET_DK), F32)
    s_cf, s_cb = _ret(logit, kc, vc, s_zero, s_zero)

    h0 = jnp.concatenate([hcf_last, hcb_first], 1)
    x1 = _ffn(x, mods_lat, *ffn1_w, lng, lnb, mod_idx=(0, 1, 2), ln_idx=0)
    k, v, q, g, gate, xc, hf, _ = _inproj(x1, mods_lat, w_in_b, tables_lat, h0=h0,
                                          **conv_w, **lru_w)
    lru_out, _ = _lru_bwd(xc, (gate, hf), h0=h0, **lru_w)
    ret_out = _ret(logit, k, v, s_cf, s_cb, mix=(q, g, norm_g, norm_b))
    return _ffn(x1, mods_lat, *ffn2_w, lng, lnb, mod_idx=(6, 7, 8), ln_idx=2,
                mix=(5, 1), mix_in=(ret_out, lru_out, w_out_b))
```

```python
import functools

import jax
import jax.numpy as jnp
import numpy as np
from jax import lax
from jax.experimental import pallas as pl
from jax.experimental.pallas import tpu as pltpu

F32 = jnp.float32
BF16 = jnp.bfloat16

D_MODEL = 1024
D_FF = 2816
N_SUB = 3
RET_WIDTH = 512
RET_HEADS = 4
RET_DK = RET_WIDTH // RET_HEADS
RET_BLOCK = 256
RET_HEADS_PER_STEP = 2
RET_UNROLL = 4
LRU_WIDTH = 512
LRU_BLOCKS = 8
LRU_C = 8.0
CONV_W = 4
CONV_PAD_LO = 2
GRID_W = 64
DEPTH = 1
MACARON = 0.5
ALPHA = (2.0 * DEPTH) ** 0.25
ROPE_BASE = 10000.0
LN_EPS = 1e-5
K_SCALE = RET_DK ** -0.5
GELU_K0 = 2.0 * (2.0 / np.pi) ** 0.5
GELU_K1 = 8.0 * 0.044715 * (2.0 / np.pi) ** 0.5

SEG = 512
K_SEG, V_SEG, X_SEG, Q_SEG, G_SEG, GATE_SEG = range(6)

LANES = 128
SUBLANES = 8
BF16_ROWS = 16
MXU_DIM = 256
VMEM_LIMIT_BYTES = 56 * 1024 * 1024

FF_CHUNK = MXU_DIM
N_FF_CHUNKS = D_FF // FF_CHUNK
TOKEN_TILE = 512
FFN_SUB_TILE = 512
ADA_COL_TILE = 1024
WEIGHT_CAST_STEPS = 8
WEIGHT_CAST_BLOCKS = 16
HALO = SUBLANES
LRU_PRE = CONV_PAD_LO * SUBLANES
LRU_EXT = (CONV_W - 1) * SUBLANES


def _layer_norm(y, g, b):
    mu = jnp.mean(y, -1, keepdims=True)
    yc = y - mu
    var = jnp.mean(yc * yc, -1, keepdims=True)
    return yc * lax.rsqrt(var + LN_EPS) * g + b


def _sigmoid(x):
    return 0.5 * jnp.tanh(0.5 * x) + 0.5


def _softplus(z):
    return jnp.maximum(z, 0.0) + jnp.log(1.0 + jnp.exp(-jnp.abs(z)))


def _const_spec(shape):
    zeros = (0,) * len(shape)
    return pl.BlockSpec(shape, lambda *_: zeros, pipeline_mode=pl.Buffered(1))


def _params(n_grid_axes):
    return pltpu.CompilerParams(dimension_semantics=("arbitrary",) * n_grid_axes,
                                vmem_limit_bytes=VMEM_LIMIT_BYTES)


def _cast_kernel(*refs, scaled):
    n = len(scaled)
    ins, outs = refs[:len(refs) - n], refs[len(refs) - n:]
    src = iter(ins)
    for o_ref, has_scale in zip(outs, scaled):
        w = next(src)[...]
        if has_scale:
            w = w * next(src)[...]
        o_ref[...] = w.astype(BF16)


def _to_bf16(weights, col_scales):
    steps = WEIGHT_CAST_STEPS
    args, in_specs, out_specs, out_shape = [], [], [], []
    for w, cs in zip(weights, col_scales):
        rows, cols = w.shape
        blk = pl.BlockSpec((rows // steps, cols), lambda i: (i, 0))
        args.append(w)
        in_specs.append(blk)
        if cs is not None:
            args.append(cs)
            in_specs.append(pl.BlockSpec((1, cols), lambda i: (0, 0)))
        out_specs.append(blk)
        out_shape.append(jax.ShapeDtypeStruct(w.shape, BF16))
    return pl.pallas_call(
        functools.partial(_cast_kernel, scaled=tuple(cs is not None for cs in col_scales)),
        out_shape=out_shape, grid=(steps,), in_specs=in_specs, out_specs=out_specs,
        compiler_params=_params(1), name="cast",
    )(*args)


def _ada_kernel(c_ref, w_ref, b_ref, o_ref):
    c = c_ref[...]
    s = c * _sigmoid(c)
    o_ref[...] = jnp.dot(s, w_ref[...], preferred_element_type=F32) + b_ref[...]


def _ada(cc, w, b):
    rows, d = cc.shape
    n = w.shape[1]
    tn = ADA_COL_TILE
    return pl.pallas_call(
        _ada_kernel,
        out_shape=jax.ShapeDtypeStruct((rows, n), F32),
        grid=(n // tn,),
        in_specs=[pl.BlockSpec((rows, d), lambda j: (0, 0)),
                  pl.BlockSpec((d, tn), lambda j: (0, j)),
                  pl.BlockSpec((1, tn), lambda j: (0, j))],
        out_specs=pl.BlockSpec((rows, tn), lambda j: (0, j)),
        compiler_params=_params(1),
        name="ada",
    )(cc, w, b.reshape(1, n))


def _ffn_kernel(*refs, mod_idx, ln_idx, mix, tm, n_cast):
    n_in = 7 + (3 if mix is not None else 0)
    if mix is None:
        x_ref, mod_ref, wg_ref, wu_ref, wd_ref, lng_ref, lnb_ref = refs[:n_in]
    else:
        (x_ref, ret_ref, lru_ref, wo_ref, mod_ref, wg_ref, wu_ref, wd_ref, lng_ref,
         lnb_ref) = refs[:n_in]
    o_ref = refs[n_in + n_cast]
    a_scr = refs[-1]
    if n_cast:
        step = pl.program_id(0) * pl.num_programs(1) + pl.program_id(1)
        steps_per_block = pl.num_programs(0) * pl.num_programs(1) // WEIGHT_CAST_BLOCKS

        @pl.when(step % steps_per_block == 0)
        def _():
            for src, dst in zip(refs[n_in:n_in + n_cast], refs[n_in + n_cast + 1:-1]):
                dst[...] = src[...].astype(BF16)
    m = mod_ref[0]
    i_shift, i_scale, i_gate = mod_idx
    sub = min(tm, FFN_SUB_TILE)
    for st in range(tm // sub):
        rows = slice(st * sub, (st + 1) * sub)
        h = x_ref[0, rows, :]
        if mix is not None:
            gate_idx, mix_ln_idx = mix
            z = jnp.concatenate([ret_ref[0, rows, :], lru_ref[0, rows, :]], -1)
            y = jnp.dot(z, wo_ref[...], preferred_element_type=F32)
            h = _layer_norm(ALPHA * h + m[gate_idx:gate_idx + 1] * y,
                            lng_ref[mix_ln_idx:mix_ln_idx + 1], lnb_ref[mix_ln_idx:mix_ln_idx + 1])
        u = (h * (1.0 + m[i_scale:i_scale + 1]) + m[i_shift:i_shift + 1]).astype(BF16)
        for j in range(N_FF_CHUNKS):
            cols = slice(j * FF_CHUNK, (j + 1) * FF_CHUNK)
            g = jnp.dot(u, wg_ref[:, cols], preferred_element_type=F32)
            p = jnp.dot(u, wu_ref[:, cols], preferred_element_type=F32)
            a_scr[rows, cols] = (g * _sigmoid(g) * p).astype(BF16)
        f = jnp.dot(a_scr[rows, :], wd_ref[...], preferred_element_type=F32)
        y = ALPHA * h + (MACARON * m[i_gate:i_gate + 1]) * f
        o_ref[0, rows, :] = _layer_norm(y, lng_ref[ln_idx:ln_idx + 1], lnb_ref[ln_idx:ln_idx + 1])


def _ffn(x, mods, wg, wu, wd, ln_g, ln_b, *, mod_idx, ln_idx, mix=None, mix_in=None, cast_along=()):
    bsz, t, d = x.shape
    tm = min(TOKEN_TILE, t)
    nt = t // tm
    tok = lambda w: pl.BlockSpec((1, tm, w), lambda b, i: (b, i, 0))
    in_specs = [tok(d)]
    args = [x]
    if mix is not None:
        ret_out, lru_out, w_out = mix_in
        in_specs += [tok(RET_WIDTH), tok(LRU_WIDTH), _const_spec(w_out.shape)]
        args += [ret_out, lru_out, w_out]
    in_specs += [pl.BlockSpec((1, 3 * N_SUB, d), lambda b, i: (b, 0, 0)),
                 _const_spec(wg.shape), _const_spec(wu.shape), _const_spec(wd.shape),
                 _const_spec(ln_g.shape), _const_spec(ln_b.shape)]
    args += [mods, wg, wu, wd, ln_g, ln_b]
    out_specs, out_shape = [tok(d)], [jax.ShapeDtypeStruct((bsz, t, d), F32)]
    for w in cast_along:
        blk = pl.BlockSpec((w.shape[0] // WEIGHT_CAST_BLOCKS, w.shape[1]),
                           lambda b, i: ((b * nt + i) * WEIGHT_CAST_BLOCKS // (bsz * nt), 0))
        args.append(w)
        in_specs.append(blk)
        out_specs.append(blk)
        out_shape.append(jax.ShapeDtypeStruct(w.shape, BF16))
    outs = pl.pallas_call(
        functools.partial(_ffn_kernel, mod_idx=mod_idx, ln_idx=ln_idx, mix=mix, tm=tm,
                          n_cast=len(cast_along)),
        out_shape=out_shape,
        grid=(bsz, nt),
        in_specs=in_specs,
        out_specs=out_specs,
        scratch_shapes=[pltpu.VMEM((tm, D_FF), BF16)],
        compiler_params=_params(2),
        name="ffn_mix" if mix is not None else "ffn",
    )(*args)
    return (outs[0], outs[1:]) if cast_along else outs[0]


def _lru_conv_slab(sl, xl, before, after, cw_ref, cb_ref, xe_scr, tm):
    seg = tm // SUBLANES
    pre = LRU_PRE
    ln = slice(sl * LANES, (sl + 1) * LANES)
    row1 = lax.broadcasted_iota(jnp.int32, (SUBLANES, LANES), 0)
    for s in range(SUBLANES):
        xe_scr[sl, pl.ds(pre + s, seg, stride=SUBLANES), :] = xl[s * seg:(s + 1) * seg, ln]
    for d in range(CONV_PAD_LO):
        src = pre + (seg - 1 - d) * SUBLANES
        edge = jnp.broadcast_to(before[HALO - 1 - d:HALO - d, ln], (SUBLANES, LANES))
        xe_scr[sl, pre - (d + 1) * SUBLANES:pre - d * SUBLANES, :] = jnp.where(
            row1 == 0, edge, pltpu.roll(xe_scr[sl, src:src + SUBLANES, :], 1, 0))
    edge = jnp.broadcast_to(after[0:1, ln], (SUBLANES, LANES))
    xe_scr[sl, pre + tm:pre + tm + SUBLANES, :] = jnp.where(
        row1 == SUBLANES - 1, edge, pltpu.roll(xe_scr[sl, pre:pre + SUBLANES, :], SUBLANES - 1, 0))
    acc = cb_ref[:, ln]
    for k in range(CONV_W):
        off = pre + (k - CONV_PAD_LO) * SUBLANES
        acc = acc + xe_scr[sl, off:off + tm, :] * cw_ref[k:k + 1, ln]
    return acc


def _lru_gates_slab(xc, w_gates):
    return jnp.dot(xc.astype(BF16), w_gates, preferred_element_type=F32)


def _lru_coeffs_slab(sl, xc, gi, bias, lam, a_scr, b_scr):
    th = jnp.tanh(gi + bias)
    th_r = th[:, :LANES]
    th_i = th[:, LANES:]
    c = (-0.5 * LRU_C) * _softplus(-lam)
    log_a = c * th_r + c
    a = jnp.exp(log_a)
    a_scr[sl] = a
    t4 = (-0.25 * jnp.tanh(log_a)) * (1.0 + a * a)
    half_root = jnp.where(t4 > 0.0, t4 * lax.rsqrt(t4), 0.0)
    b_scr[sl] = (half_root * xc) * (th_i + 1.0)


def _sublane_scan(a, b, hprev, reverse):
    row = lax.broadcasted_iota(jnp.int32, a.shape, 0)
    for k in (1, 2, 4):
        if reverse:
            sh = SUBLANES - k
            msk = row < sh
        else:
            sh = k
            msk = row >= k
        a_s = pltpu.roll(a, sh, 0)
        b_s = pltpu.roll(b, sh, 0)
        b = jnp.where(msk, a * b_s + b, b)
        a = jnp.where(msk, a * a_s, a)
    return a * hprev + b


def _lru_scan_slab(sl, a_scr, b_scr, carry_in, reverse, tm):
    seg = tm // SUBLANES
    row = lax.broadcasted_iota(jnp.int32, (SUBLANES, LANES), 0)
    hh = jnp.zeros((SUBLANES, LANES), F32)
    pp = jnp.ones((SUBLANES, LANES), F32)
    for jj in range(seg):
        j = seg - 1 - jj if reverse else jj
        rows = slice(j * SUBLANES, (j + 1) * SUBLANES)
        aj = a_scr[sl, rows, :]
        hh = aj * hh + b_scr[sl, rows, :]
        pp = aj * pp
        b_scr[sl, rows, :] = hh
        a_scr[sl, rows, :] = pp
    ends = _sublane_scan(pp, hh, carry_in, reverse)
    if reverse:
        enter = jnp.where(row == SUBLANES - 1, carry_in, pltpu.roll(ends, SUBLANES - 1, 0))
        leave = ends[0:1]
    else:
        enter = jnp.where(row == 0, carry_in, pltpu.roll(ends, 1, 0))
        leave = ends[SUBLANES - 1:SUBLANES]
    h = (b_scr[sl].reshape(seg, SUBLANES, LANES)
         + a_scr[sl].reshape(seg, SUBLANES, LANES) * enter[None]).reshape(tm, LANES)
    return h, leave


def _lru_scratch(tm):
    nl = LRU_WIDTH // LANES
    return [pltpu.VMEM((nl, tm + LRU_EXT, LANES), F32),
            pltpu.VMEM((nl, tm, LANES), F32),
            pltpu.VMEM((nl, tm, LANES), F32),
            pltpu.VMEM((SUBLANES, LRU_WIDTH), F32)]


def _inproj_kernel(x_ref, mod_ref, w_ref, cos_ref, sa_ref, sb_ref,
                   cw_ref, cb_ref, wg_ref, bias_ref, lam_ref, h0_ref, *refs, tm, nt, states_only):
    if states_only:
        k_ref, v_ref, xc_ref, hf_ref, hfl_ref = refs[:5]
    else:
        k_ref, v_ref, q_ref, g_ref, gate_ref, xc_ref, hf_ref, hfl_ref = refs[:8]
    xe_scr, a_scr, b_scr, carry_scr, xl_scr, tail_scr = refs[-6:]
    s = pl.program_id(0)
    ip = (jnp.maximum(s, 1) - 1) % nt

    @pl.when(s == 0)
    def _():
        xl_scr[...] = jnp.zeros_like(xl_scr)
        tail_scr[...] = jnp.zeros_like(tail_scr)
        carry_scr[...] = jnp.zeros_like(carry_scr)

    m = mod_ref[0]
    u = (x_ref[0] * (1.0 + m[4:5]) + m[3:4]).astype(BF16)
    cos = cos_ref[...]
    sa = sa_ref[...]
    sb = sb_ref[...]

    def seg(j):
        return jnp.dot(u, w_ref[:, j * SEG:(j + 1) * SEG], preferred_element_type=F32)

    def rope(t):
        outs = []
        for hd in range(RET_HEADS):
            th = t[:, hd * RET_DK:(hd + 1) * RET_DK]
            outs.append(th * cos + pltpu.roll(th, RET_DK - 32, 1) * sa + pltpu.roll(th, 32, 1) * sb)
        return jnp.concatenate(outs, -1)

    xl_new = seg(X_SEG)

    nl = LRU_WIDTH // LANES
    xl_old = xl_scr[...]
    before = jnp.where(ip > 0, tail_scr[...], 0.0)
    after = jnp.where(ip < nt - 1, xl_new[0:HALO], 0.0)
    carry_in = jnp.where(ip == 0, jnp.broadcast_to(h0_ref[0, 0:1], carry_scr.shape), carry_scr[...])

    xc = [_lru_conv_slab(sl, xl_old, before, after, cw_ref, cb_ref, xe_scr, tm) for sl in range(nl)]
    for sl in range(nl):
        xc_ref[0, :, sl * LANES:(sl + 1) * LANES] = xc[sl]
    tail_scr[...] = xl_old[tm - HALO:tm]
    xl_scr[...] = xl_new
    gi = [_lru_gates_slab(xc[sl], wg_ref[0, sl]) for sl in range(nl)]

    k_ref[0] = (rope(seg(K_SEG)) * K_SCALE).astype(BF16)
    v_ref[0] = seg(V_SEG).astype(BF16)
    if not states_only:
        q_ref[0] = rope(seg(Q_SEG)).astype(BF16)
        g_ref[0] = seg(G_SEG)
        gate_ref[0] = seg(GATE_SEG)

    leaves = []
    for sl in range(nl):
        ln = slice(sl * LANES, (sl + 1) * LANES)
        _lru_coeffs_slab(sl, xc[sl], gi[sl], bias_ref[0, sl], lam_ref[0, :, ln], a_scr, b_scr)
        h, leave = _lru_scan_slab(sl, a_scr, b_scr, carry_in[:, ln], False, tm)
        hf_ref[0, :, ln] = h
        leaves.append(leave)
    leave = jnp.concatenate(leaves, -1)
    carry_scr[...] = jnp.broadcast_to(leave, carry_scr.shape)

    @pl.when(ip == nt - 1)
    def _():
        hfl_ref[0] = leave


def _inproj(x, mods, w_in, tables, conv_w, conv_b, w_gates, bias, lam, h0, states_only=False):
    bsz, t, d = x.shape
    tm = min(TOKEN_TILE, t)
    nt = t // tm
    ntiles = bsz * nt
    cur = lambda s: jnp.minimum(s, ntiles - 1)
    old = lambda s: jnp.maximum(s, 1) - 1
    tok = lambda w: pl.BlockSpec((1, tm, w), lambda s: (cur(s) // nt, cur(s) % nt, 0))
    tab = pl.BlockSpec((tm, RET_DK), lambda s: (cur(s) % nt, 0))
    shp = lambda dt: jax.ShapeDtypeStruct((bsz, t, SEG), dt)
    cur_dtypes = [BF16, BF16] if states_only else [BF16, BF16, BF16, F32, F32]
    return pl.pallas_call(
        functools.partial(_inproj_kernel, tm=tm, nt=nt, states_only=states_only),
        out_shape=[shp(dt) for dt in cur_dtypes] + [shp(F32), shp(F32),
                                                    jax.ShapeDtypeStruct((bsz, 1, LRU_WIDTH), F32)],
        grid=(ntiles + 1,),
        in_specs=[tok(d), pl.BlockSpec((1, 3 * N_SUB, d), lambda s: (cur(s) // nt, 0, 0)),
                  _const_spec(w_in.shape), tab, tab, tab,
                  _const_spec(conv_w.shape), _const_spec(conv_b.shape), _const_spec(w_gates.shape),
                  _const_spec(bias.shape), _const_spec(lam.shape),
                  pl.BlockSpec((1, 2, LRU_WIDTH), lambda s: (old(s) // nt, 0, 0))],
        out_specs=[tok(SEG)] * len(cur_dtypes)
        + [pl.BlockSpec((1, tm, LRU_WIDTH), lambda s: (old(s) // nt, old(s) % nt, 0))] * 2
        + [pl.BlockSpec((1, 1, LRU_WIDTH), lambda s: (old(s) // nt, 0, 0))],
        scratch_shapes=_lru_scratch(tm) + [pltpu.VMEM((tm, LRU_WIDTH), F32),
                                           pltpu.VMEM((HALO, LRU_WIDTH), F32)],
        compiler_params=_params(1),
        name="inproj",
    )(x, mods, w_in, *tables, conv_w, conv_b, w_gates, bias, lam, h0)


def _lru_bwd_kernel(*refs, tm, nt, state_only):
    if state_only:
        xc_ref, wg_ref, bias_ref, lam_ref, h0_ref, hbf_ref, a_scr, b_scr, carry_scr = refs
    else:
        (xc_ref, gate_ref, hf_ref, wg_ref, bias_ref, lam_ref, h0_ref,
         o_ref, hbf_ref, a_scr, b_scr, carry_scr, out_scr) = refs
    i = pl.program_id(1)
    nl = LRU_WIDTH // LANES
    seg = tm // SUBLANES

    @pl.when(i == 0)
    def _():
        carry_scr[...] = jnp.broadcast_to(h0_ref[0, 1:2], carry_scr.shape)

    carry_in = carry_scr[...]
    leaves = []
    for sl in range(nl):
        ln = slice(sl * LANES, (sl + 1) * LANES)
        xc = xc_ref[0, :, ln]
        gi = _lru_gates_slab(xc, wg_ref[1, sl])
        _lru_coeffs_slab(sl, xc, gi, bias_ref[1, sl], lam_ref[1, :, ln], a_scr, b_scr)
        h, leave = _lru_scan_slab(sl, a_scr, b_scr, carry_in[:, ln], True, tm)
        if not state_only:
            out_scr[sl] = hf_ref[0, :, ln] + h
        leaves.append(leave)
    leave = jnp.concatenate(leaves, -1)
    carry_scr[...] = jnp.broadcast_to(leave, carry_scr.shape)

    @pl.when(i == nt - 1)
    def _():
        hbf_ref[0] = leave

    if state_only:
        return
    for s in range(SUBLANES):
        ht = jnp.concatenate([out_scr[sl, pl.ds(s, seg, stride=SUBLANES), :]
                              for sl in range(nl)], -1)
        rows = slice(s * seg, (s + 1) * seg)
        hx = gate_ref[0, rows, :]
        gelu = hx * (1.0 + jnp.tanh(hx * (GELU_K0 + GELU_K1 * (hx * hx))))
        o_ref[0, rows, :] = (ht * gelu).astype(BF16)


def _lru_bwd(xc, gate_hf, w_gates, bias, lam, h0):
    bsz, t, w = xc.shape
    tm = min(TOKEN_TILE, t)
    nt = t // tm
    state_only = gate_hf is None
    tok = pl.BlockSpec((1, tm, w), lambda b, i: (b, nt - 1 - i, 0))
    state = pl.BlockSpec((1, 1, w), lambda b, i: (b, 0, 0))
    st_shape = jax.ShapeDtypeStruct((bsz, 1, w), F32)
    consts = [_const_spec(w_gates.shape), _const_spec(bias.shape), _const_spec(lam.shape),
              pl.BlockSpec((1, 2, w), lambda b, i: (b, 0, 0))]
    scratch = _lru_scratch(tm)[1:]
    if state_only:
        args, in_specs, out_specs, out_shape = [xc], [tok], state, st_shape
    else:
        args, in_specs = [xc, *gate_hf], [tok, tok, tok]
        out_specs, out_shape = [tok, state], [jax.ShapeDtypeStruct((bsz, t, w), BF16), st_shape]
        scratch = scratch + [pltpu.VMEM((w // LANES, tm, LANES), F32)]
    return pl.pallas_call(
        functools.partial(_lru_bwd_kernel, tm=tm, nt=nt, state_only=state_only),
        out_shape=out_shape, grid=(bsz, nt), in_specs=in_specs + consts, out_specs=out_specs,
        scratch_shapes=scratch, compiler_params=_params(2), name="lru_bwd",
    )(*args, w_gates, bias, lam, h0)


def _ret_kernel(*refs, nchunks, states_only):
    if states_only:
        logit_ref, k_ref, v_ref, s0f_ref, s0b_ref, sff_ref, sbf_ref, kv_scr = refs
    else:
        (logit_ref, k_ref, v_ref, s0f_ref, s0b_ref, q_ref, g_ref, ng_ref, nb_ref,
         o_ref, kv_scr, dmat_scr, ss_scr) = refs
    hp = pl.program_id(1)
    c = RET_BLOCK
    dk = RET_DK
    contract_cols = (((1,), (1,)), ((), ()))

    def rows(n):
        return pl.ds(pl.multiple_of(n * c, c), c)

    def log_decay(direction, hd, shape):
        z = jnp.full(shape, logit_ref[direction, hd], F32)
        return -_softplus(-z)

    rowi = lax.broadcasted_iota(jnp.int32, (c, c), 0)
    coli = lax.broadcasted_iota(jnp.int32, (c, c), 1)
    diff = (rowi - coli).astype(F32)
    pos = lax.broadcasted_iota(jnp.int32, (c, dk), 0).astype(F32)
    lpos = lax.broadcasted_iota(jnp.int32, (dk, c), 1).astype(F32)

    heads = []
    for hh in range(RET_HEADS_PER_STEP):
        hd = hp * RET_HEADS_PER_STEP + hh
        if not states_only:
            lgf = log_decay(0, hd, (c, c))
            lgb = log_decay(1, hd, (c, c))
            dmat_scr[hh] = jnp.where(diff >= 0, jnp.exp(lgf * jnp.maximum(diff, 0.0)),
                                     jnp.exp(lgb * jnp.maximum(-diff, 0.0))).astype(BF16)
        lgf_r, lgb_r = log_decay(0, hd, (c, dk)), log_decay(1, hd, (c, dk))
        heads.append(dict(
            lanes=slice(hh * dk, (hh + 1) * dk),
            qdec_f=jnp.exp(lgf_r * (pos + 1.0)).astype(BF16),
            qdec_b=jnp.exp(lgb_r * (c - pos)).astype(BF16),
            kdec_f=jnp.exp(log_decay(0, hd, (dk, c)) * (c - 1.0 - lpos)),
            kdec_b=jnp.exp(log_decay(1, hd, (dk, c)) * lpos),
            gc_f=jnp.exp(log_decay(0, hd, (dk, dk)) * float(c)),
            gc_b=jnp.exp(log_decay(1, hd, (dk, dk)) * float(c))))

    def pass_a(n, carry):
        for hh, hv in enumerate(heads):
            kt = k_ref[0, rows(n), hv["lanes"]].astype(F32).T
            lhs = jnp.concatenate([kt * hv["kdec_f"], kt * hv["kdec_b"]], 0).astype(BF16)
            kv_scr[hh, n] = jnp.dot(lhs, v_ref[0, rows(n), hv["lanes"]],
                                    preferred_element_type=F32)
        return carry

    lax.fori_loop(0, nchunks, pass_a, 0, unroll=min(nchunks, RET_UNROLL))

    for hh, hv in enumerate(heads):
        def pass_s(n, carry, hh=hh, hv=hv):
            sf, sb = carry
            m = nchunks - 1 - n
            if not states_only:
                ss_scr[hh, n, 0:dk] = sf.astype(BF16)
                ss_scr[hh, m, dk:2 * dk] = sb.astype(BF16)
            return (hv["gc_f"] * sf + kv_scr[hh, n, 0:dk],
                    hv["gc_b"] * sb + kv_scr[hh, m, dk:2 * dk])

        sf, sb = lax.fori_loop(0, nchunks, pass_s, (s0f_ref[0, hh], s0b_ref[0, hh]))
        if states_only:
            sff_ref[0, hh] = sf
            sbf_ref[0, hh] = sb

    if states_only:
        return
    gn = ng_ref[...]
    bn = nb_ref[...]

    def pass_b(n, carry):
        for hh, hv in enumerate(heads):
            ln = hv["lanes"]
            qn = q_ref[0, rows(n), ln]
            sc = lax.dot_general(qn, k_ref[0, rows(n), ln], contract_cols,
                                 preferred_element_type=F32)
            o = jnp.dot(sc.astype(BF16) * dmat_scr[hh], v_ref[0, rows(n), ln],
                        preferred_element_type=F32)
            qd = jnp.concatenate([qn * hv["qdec_f"], qn * hv["qdec_b"]], -1)
            o = o + jnp.dot(qd, ss_scr[hh, n], preferred_element_type=F32)
            mu = jnp.mean(o, -1, keepdims=True)
            oc = o - mu
            var = jnp.mean(oc * oc, -1, keepdims=True)
            on = oc * lax.rsqrt(var + LN_EPS) * gn[:, ln] + bn[:, ln]
            hg = g_ref[0, rows(n), ln]
            o_ref[0, rows(n), ln] = (on * (hg * (jnp.tanh(hg) + 1.0))).astype(BF16)
        return carry

    lax.fori_loop(0, nchunks, pass_b, 0, unroll=min(nchunks, RET_UNROLL))


def _ret(logit, k, v, s0f, s0b, mix=None):
    bsz, t, _ = k.shape
    nchunks = t // RET_BLOCK
    hps = RET_HEADS_PER_STEP
    wid = hps * RET_DK
    states_only = mix is None
    head = pl.BlockSpec((1, t, wid), lambda b, h: (b, 0, h))
    vec = pl.BlockSpec((1, wid), lambda b, h: (0, h))
    st = pl.BlockSpec((1, hps, RET_DK, RET_DK), lambda b, h: (b, h, 0, 0))
    st_shape = jax.ShapeDtypeStruct((bsz, RET_HEADS, RET_DK, RET_DK), F32)
    args = [logit, k, v, s0f, s0b]
    in_specs = [pl.BlockSpec(memory_space=pltpu.SMEM), head, head, st, st]
    scratch = [pltpu.VMEM((hps, nchunks, 2 * RET_DK, RET_DK), F32)]
    if states_only:
        out_specs, out_shape = [st, st], [st_shape, st_shape]
    else:
        args += list(mix)
        in_specs += [head, head, vec, vec]
        out_specs, out_shape = head, jax.ShapeDtypeStruct((bsz, t, RET_WIDTH), BF16)
        scratch += [pltpu.VMEM((hps, RET_BLOCK, RET_BLOCK), BF16),
                    pltpu.VMEM((hps, nchunks, 2 * RET_DK, RET_DK), BF16)]
    return pl.pallas_call(
        functools.partial(_ret_kernel, nchunks=nchunks, states_only=states_only),
        out_shape=out_shape, grid=(bsz, RET_HEADS // hps), in_specs=in_specs, out_specs=out_specs,
        scratch_shapes=scratch, compiler_params=_params(2), name="ret",
    )(*args)


def _rope_tables(t):
    n = RET_DK // 4
    inv = ROPE_BASE ** (-np.arange(n, dtype=np.float64) / n)
    pos = np.arange(t)
    a_r = (pos // GRID_W)[:, None] * inv
    a_c = (pos % GRID_W)[:, None] * inv
    ang = np.concatenate([a_r, a_r, a_c, a_c], -1)
    first = (np.arange(RET_DK) % (2 * n)) < n
    sin = np.sin(ang)
    tabs = (np.cos(ang), np.where(first, -sin, 0.0), np.where(first, 0.0, sin))
    return tuple(jnp.asarray(tab, F32) for tab in tabs)


def _block_diag(wb):
    nb, c, d = wb.shape
    eye = jnp.eye(nb, dtype=wb.dtype)
    return jnp.einsum('ncd,nm->ncmd', wb, eye).reshape(nb * c, nb * d)


def kernel(x, c, ctx, c_ctx, w_ada, b_ada, ffn1_w_gate, ffn1_w_up, ffn1_w_down, ffn2_w_gate,
           ffn2_w_up, ffn2_w_down, w_in, w_out, ret_decay_logit, ret_norm_g, ret_norm_b,
           lru_conv_w, lru_conv_b, lru_w_a, lru_b_a, lru_w_i, lru_b_i, lru_lambda, ln_g, ln_b):
    bsz, t, d = x.shape
    tc = ctx.shape[1]
    l = 0

    pad = jnp.zeros((2 * SUBLANES - bsz - 1, d), F32)
    m = _ada(jnp.concatenate([c, c_ctx[None], pad], 0), w_ada[l], b_ada[l])
    mods_lat = m[:bsz].reshape(bsz, 3 * N_SUB, d)
    mods_ctx = jnp.broadcast_to(m[bsz].reshape(1, 3 * N_SUB, d), (bsz, 3 * N_SUB, d))

    col_scale = jnp.ones((1, w_in.shape[-1]), F32).at[:, G_SEG * SEG:(GATE_SEG + 1) * SEG].set(0.5)
    weights = _to_bf16([ffn1_w_gate[l], ffn1_w_up[l], ffn1_w_down[l], w_in[l]],
                       [None] * 3 + [col_scale])
    ffn1_w, w_in_b = weights[0:3], weights[3]
    late_weights = (ffn2_w_gate[l], ffn2_w_up[l], ffn2_w_down[l], w_out[l])
    lng, lnb = ln_g[l], ln_b[l]

    nl = LRU_WIDTH // LANES
    per_slab = LRU_BLOCKS // nl

    def slab_blocks(wb):
        wb = wb.reshape(2 * nl, per_slab, wb.shape[-2], wb.shape[-1])
        return jax.vmap(_block_diag)(wb).reshape(2, nl, LANES, LANES)

    w_gates = (0.5 * jnp.concatenate([slab_blocks(lru_w_a[l]), slab_blocks(lru_w_i[l])], -1)
               ).astype(BF16)
    gate_bias = 0.5 * jnp.concatenate([lru_b_a[l].reshape(2, nl, 1, LANES),
                                       lru_b_i[l].reshape(2, nl, 1, LANES)], -1)
    lam = lru_lambda[l][:, None, :]
    lru_w = dict(w_gates=w_gates, bias=gate_bias, lam=lam)
    conv_w = dict(conv_w=lru_conv_w[l], conv_b=lru_conv_b[l][None, :])
    norm_g = ret_norm_g[l][None, :]
    norm_b = ret_norm_b[l][None, :]
    logit = ret_decay_logit[l]

    tables_lat = _rope_tables(t)
    tables_ctx = (jnp.ones((tc, RET_DK), F32), jnp.zeros((tc, RET_DK), F32),
                  jnp.zeros((tc, RET_DK), F32))

    rows_per_tile = max(TOKEN_TILE // tc, 1)
    hc = _ffn(ctx.reshape(bsz // rows_per_tile, rows_per_tile * tc, d), mods_ctx[:bsz // rows_per_tile],
              *ffn1_w, lng, lnb, mod_idx=(0, 1, 2), ln_idx=0).reshape(bsz, tc, d)
    h0_zero = jnp.zeros((bsz, 2, LRU_WIDTH), F32)
    kc, vc, xcc, _, hcf_last = _inproj(hc, mods_ctx, w_in_b, tables_ctx, h0=h0_zero,
                                       states_only=True, **conv_w, **lru_w)
    hcb_first = _lru_bwd(xcc, None, h0=h0_zero, **lru_w)
    s_zero = jnp.zeros((bsz, RET_HEADS, RET_DK, RET_DK), F32)
    s_cf, s_cb = _ret(logit, kc, vc, s_zero, s_zero)

    h0 = jnp.concatenate([hcf_last, hcb_first], 1)
    x1, late_bf16 = _ffn(x, mods_lat, *ffn1_w, lng, lnb, mod_idx=(0, 1, 2), ln_idx=0,
                         cast_along=late_weights)
    ffn2_w, w_out_b = late_bf16[0:3], late_bf16[3]
    k, v, q, g, gate, xc, hf, _ = _inproj(x1, mods_lat, w_in_b, tables_lat, h0=h0,
                                          **conv_w, **lru_w)
    lru_out, _ = _lru_bwd(xc, (gate, hf), h0=h0, **lru_w)
    ret_out = _ret(logit, k, v, s_cf, s_cb, mix=(q, g, norm_g, norm_b))
    return _ffn(x1, mods_lat, *ffn2_w, lng, lnb, mod_idx=(6, 7, 8), ln_idx=2,
                mix=(5, 1), mix_in=(ret_out, lru_out, w_out_b))
```
